```python
import jax, jax.numpy as jnp
from jax import lax
import numpy as np

D_MODEL = 4096
BATCH = 4
SEQ = 4096
DEPTH = 1

D_RNN = 2048
LRU_HEADS = 16
LRU_BLOCK = D_RNN // LRU_HEADS
CONV_W = 4
LRU_C = 8.0
N_HEADS = 16
HEAD_DIM = 128
D_ATTN = N_HEADS * HEAD_DIM
GRID_W = 64
KH_MAX = 8
KW = 16
QB = 16
KB = QB + KW
PEER_HEADS = 8
N_KEYS = 128
N_EXPERTS = N_KEYS * N_KEYS
D_KEY = 256
D_HALF = D_KEY // 2
TOPK = 16
TOKEN_BLOCK = 64
ALPHA = (2 * DEPTH) ** 0.25
BETA = (8 * DEPTH) ** -0.25
LN_EPS = 1e-5
S1 = D_RNN
S2 = 2 * D_RNN
S3 = S2 + D_ATTN
S4 = S3 + D_ATTN
S5 = S4 + D_ATTN
S6 = S5 + D_MODEL
D_IN = S6 + D_MODEL

kernel_name = "hybrid_rglru_natten_peer_deepnorm"


def layer_norm(x, g, b):
    xf = x.astype(jnp.float32)
    mu = jnp.mean(xf, axis=-1, keepdims=True)
    var = jnp.mean(jnp.square(xf - mu), axis=-1, keepdims=True)
    y = (xf - mu) * lax.rsqrt(var + LN_EPS) * g.astype(jnp.float32) + b.astype(jnp.float32)
    return y.astype(x.dtype)


def centred_depthwise_conv(x, w, b):
    c = x.shape[-1]
    y = lax.conv_general_dilated(
        x, w[:, None, :].astype(x.dtype), window_strides=(1,),
        padding=[(CONV_W // 2, CONV_W - 1 - CONV_W // 2)],
        dimension_numbers=("NWC", "WIO", "NWC"), feature_group_count=c)
    return y + b.astype(x.dtype)


def _lin_combine(e1, e2):
    a1, b1 = e1
    a2, b2 = e2
    return a1 * a2, a2 * b1 + b2


def rg_lru(xc, wa, ba, wx, bx, lam, reverse):
    bn, s, _ = xc.shape
    xf = xc.astype(jnp.float32)
    xh = xf.reshape(bn, s, LRU_HEADS, LRU_BLOCK)
    r = jax.nn.sigmoid(jnp.einsum("bshi,hij->bshj", xh, wa.astype(jnp.float32)).reshape(bn, s, D_RNN) + ba.astype(jnp.float32))
    i = jax.nn.sigmoid(jnp.einsum("bshi,hij->bshj", xh, wx.astype(jnp.float32)).reshape(bn, s, D_RNN) + bx.astype(jnp.float32))
    log_a = -LRU_C * r * jax.nn.softplus(-lam.astype(jnp.float32))
    a = jnp.exp(log_a)
    u = jnp.sqrt(-jnp.expm1(2.0 * log_a)) * (i * xf)
    _, h = lax.associative_scan(_lin_combine, (a, u), reverse=reverse, axis=1)
    return h


def neighbourhood_attention(q, k, v, rpb):
    bn, s, _ = q.shape
    rows = s // GRID_W
    kh = min(KH_MAX, rows)
    ncb = GRID_W // QB

    def to_grid(t):
        return t.reshape(bn, rows, GRID_W, N_HEADS, HEAD_DIM).transpose(0, 3, 1, 2, 4)

    qg = to_grid(q * (HEAD_DIM ** -0.5))
    kg = to_grid(k)
    vg = to_grid(v)
    qcol = np.arange(GRID_W).reshape(ncb, QB)
    kc0 = np.clip(np.arange(ncb) * QB - KW // 2, 0, GRID_W - KB)
    col_idx = kc0[:, None] + np.arange(KB)[None, :]
    cs = np.clip(qcol - KW // 2, 0, GRID_W - KW)
    kc = col_idx[:, None, :]
    col_valid = jnp.asarray((kc >= cs[..., None]) & (kc < cs[..., None] + KW))
    dc_idx = np.clip(kc - qcol[..., None], -(KW - 1), KW - 1) + KW - 1
    rpb_f = rpb.astype(jnp.float32)

    def row_block(r):
        rs = jnp.clip(r - kh // 2, 0, rows - kh)
        k_band = lax.dynamic_slice_in_dim(kg, rs, kh, axis=2)[:, :, :, col_idx]
        v_band = lax.dynamic_slice_in_dim(vg, rs, kh, axis=2)[:, :, :, col_idx]
        q_row = lax.dynamic_index_in_dim(qg, r, axis=2, keepdims=False).reshape(bn, N_HEADS, ncb, QB, HEAD_DIM)
        sc = jnp.einsum("bhnqd,bhknjd->bhnqkj", q_row, k_band).astype(jnp.float32)
        dr_idx = rs + jnp.arange(kh) - r + KH_MAX - 1
        bias = rpb_f[:, dr_idx][:, :, dc_idx]
        sc = sc + bias.transpose(0, 2, 3, 1, 4)
        sc = jnp.where(col_valid[:, :, None, :], sc, -jnp.inf)
        p = jax.nn.softmax(sc.reshape(bn, N_HEADS, ncb, QB, kh * KB), axis=-1)
        p = p.reshape(sc.shape).astype(v.dtype)
        return jnp.einsum("bhnqkj,bhknjd->bhnqd", p, v_band)

    out = lax.map(row_block, jnp.arange(rows))
    return out.transpose(1, 0, 3, 4, 2, 5).reshape(bn, s, D_ATTN)


def peer_ffn(x, wq, subkeys, u_tab, v_tab):
    bn, s, d = x.shape
    t = bn * s
    xt = x.reshape(t, d)
    q = (xt @ wq).reshape(t, PEER_HEADS, 2, D_HALF)
    sc = jnp.einsum("thpd,hpkd->thpk", q, subkeys).astype(jnp.float32)
    sv, si = lax.top_k(sc, TOPK)
    cand = (sv[:, :, 0, :, None] + sv[:, :, 1, None, :]).reshape(t, PEER_HEADS, TOPK * TOPK)
    cv, ci = lax.top_k(cand, TOPK)
    i1 = jnp.take_along_axis(si[:, :, 0], ci // TOPK, axis=-1)
    i2 = jnp.take_along_axis(si[:, :, 1], ci % TOPK, axis=-1)
    nb = t // TOKEN_BLOCK
    expert = (i1 * N_KEYS + i2).reshape(nb, TOKEN_BLOCK, PEER_HEADS * TOPK)
    gate = jax.nn.softmax(cv, axis=-1).astype(x.dtype).reshape(nb, TOKEN_BLOCK, PEER_HEADS * TOPK)

    def block(args):
        xb, eb, gb = args
        u = jnp.take(u_tab, eb, axis=0)
        act = jax.nn.gelu(jnp.einsum("tkd,td->tk", u, xb), approximate=False)
        vv = jnp.take(v_tab, eb, axis=0)
        return jnp.einsum("tk,tkd->td", gb * act, vv)

    y = lax.map(block, (xt.reshape(nb, TOKEN_BLOCK, d), expert, gate))
    return y.reshape(bn, s, d)


def setup_inputs(seed: int = 0) -> dict:
    key = jax.random.key(seed)
    ks = jax.random.split(key, 24)
    nrm = jax.random.normal
    x = nrm(ks[0], (BATCH, SEQ, D_MODEL), jnp.float32)
    ln0_g = 1.0 + 0.02 * nrm(ks[1], (D_MODEL,), jnp.float32)
    ln0_b = 0.02 * nrm(ks[2], (D_MODEL,), jnp.float32)
    w_in = nrm(ks[3], (DEPTH, D_MODEL, D_IN), jnp.float32) * D_MODEL ** -0.5
    w_in = w_in.at[:, :, S4:S5].multiply(BETA)
    conv_w = nrm(ks[4], (DEPTH, CONV_W, D_RNN), jnp.float32) * CONV_W ** -0.5
    conv_b = 0.02 * nrm(ks[5], (DEPTH, D_RNN), jnp.float32)
    lru_wa = nrm(ks[6], (DEPTH, 2, LRU_HEADS, LRU_BLOCK, LRU_BLOCK), jnp.float32) * LRU_BLOCK ** -0.5
    lru_ba = 0.02 * nrm(ks[7], (DEPTH, 2, D_RNN), jnp.float32)
    lru_wx = nrm(ks[8], (DEPTH, 2, LRU_HEADS, LRU_BLOCK, LRU_BLOCK), jnp.float32) * LRU_BLOCK ** -0.5
    lru_bx = 0.02 * nrm(ks[9], (DEPTH, 2, D_RNN), jnp.float32)
    a_c = jax.random.uniform(ks[10], (DEPTH, 2, D_RNN), jnp.float32, minval=0.9, maxval=0.999)
    a = a_c ** (1.0 / LRU_C)
    lru_lambda = jnp.log(a) - jnp.log1p(-a)
    rpb = 0.02 * nrm(ks[11], (DEPTH, N_HEADS, 2 * KH_MAX - 1, 2 * KW - 1), jnp.float32)
    w_branch_rnn = nrm(ks[12], (DEPTH, D_RNN, D_MODEL), jnp.float32) * D_RNN ** -0.5
    w_branch_attn = nrm(ks[13], (DEPTH, D_ATTN, D_MODEL), jnp.float32) * D_ATTN ** -0.5
    w_out = nrm(ks[14], (DEPTH, D_MODEL, D_MODEL), jnp.float32) * (D_MODEL ** -0.5) * BETA
    ln1_g = 1.0 + 0.02 * nrm(ks[15], (DEPTH, D_MODEL), jnp.float32)
    ln1_b = 0.02 * nrm(ks[16], (DEPTH, D_MODEL), jnp.float32)
    peer_wq = nrm(ks[17], (DEPTH, D_MODEL, PEER_HEADS * D_KEY), jnp.float32) * D_MODEL ** -0.5
    peer_subkeys = nrm(ks[18], (DEPTH, PEER_HEADS, 2, N_KEYS, D_HALF), jnp.float32) * D_HALF ** -0.5
    peer_u = nrm(ks[19], (DEPTH, N_EXPERTS, D_MODEL), jnp.float32) * D_MODEL ** -0.5
    peer_v = nrm(ks[20], (DEPTH, N_EXPERTS, D_MODEL), jnp.float32) * BETA
    ln2_g = 1.0 + 0.02 * nrm(ks[21], (DEPTH, D_MODEL), jnp.float32)
    ln2_b = 0.02 * nrm(ks[22], (DEPTH, D_MODEL), jnp.float32)
    return {"x": x, "ln0_g": ln0_g, "ln0_b": ln0_b, "w_in": w_in, "conv_w": conv_w, "conv_b": conv_b,
            "lru_wa": lru_wa, "lru_ba": lru_ba, "lru_wx": lru_wx, "lru_bx": lru_bx, "lru_lambda": lru_lambda,
            "rpb": rpb, "w_branch_rnn": w_branch_rnn, "w_branch_attn": w_branch_attn, "w_out": w_out,
            "ln1_g": ln1_g, "ln1_b": ln1_b, "peer_wq": peer_wq, "peer_subkeys": peer_subkeys,
            "peer_u": peer_u, "peer_v": peer_v, "ln2_g": ln2_g, "ln2_b": ln2_b}


def reference(x, ln0_g, ln0_b, w_in, conv_w, conv_b, lru_wa, lru_ba, lru_wx, lru_bx, lru_lambda,
              rpb, w_branch_rnn, w_branch_attn, w_out, ln1_g, ln1_b, peer_wq, peer_subkeys,
              peer_u, peer_v, ln2_g, ln2_b):
    h = layer_norm(x, ln0_g, ln0_b)
    for l in range(DEPTH):
        z = h @ w_in[l]
        xr, gr, q, k, v, g_rnn, g_attn = jnp.split(z, [S1, S2, S3, S4, S5, S6], axis=-1)
        xc = centred_depthwise_conv(xr, conv_w[l], conv_b[l])
        hr = (rg_lru(xc, lru_wa[l, 0], lru_ba[l, 0], lru_wx[l, 0], lru_bx[l, 0], lru_lambda[l, 0], False)
              + rg_lru(xc, lru_wa[l, 1], lru_ba[l, 1], lru_wx[l, 1], lru_bx[l, 1], lru_lambda[l, 1], True))
        y_rnn = (jax.nn.gelu(gr, approximate=False) * hr.astype(h.dtype)) @ w_branch_rnn[l]
        y_attn = neighbourhood_attention(q, k, v, rpb[l]) @ w_branch_attn[l]
        mixed = jax.nn.sigmoid(g_rnn) * y_rnn + jax.nn.sigmoid(g_attn) * y_attn
        h = layer_norm(ALPHA * h + mixed @ w_out[l], ln1_g[l], ln1_b[l])
        y_ffn = peer_ffn(h, peer_wq[l], peer_subkeys[l], peer_u[l], peer_v[l])
        h = layer_norm(ALPHA * h + y_ffn, ln2_g[l], ln2_b[l])
    return h
```

```python
import functools
import math

import numpy as np
import jax
import jax.numpy as jnp
from jax import lax
from jax.experimental import pallas as pl
from jax.experimental.pallas import tpu as pltpu

LRU_BLOCK = 128
LRU_C = 8.0
CONV_W = 4
HEAD_DIM = 128
GRID_W = 64
KH = 8
KW = 16
PEER_HEADS = 8
N_KEYS = 128
TOPK = 16
D_HALF = 128
LN_EPS = 1e-5
DEPTH = 1
ALPHA = (2 * DEPTH) ** 0.25

V7X_VMEM_LIMIT_BYTES = 56 * 1024 * 1024
SUBLANES = 8
LANES = 128

NEG_BIG = -1e30
SQRT_HALF = 0.7071067811865476


def _params(n_axes, vmem=V7X_VMEM_LIMIT_BYTES):
    return pltpu.CompilerParams(dimension_semantics=("arbitrary",) * n_axes, vmem_limit_bytes=vmem)


def _gelu(x):
    return 0.5 * x * (1.0 + lax.erf(x * SQRT_HALF))


def _ln_body(*refs, alpha, has_res, has_bf16):
    if has_res:
        h_ref, r_ref, g_ref, b_ref = refs[:4]
        outs = refs[4:]
        x = alpha * h_ref[...] + r_ref[...]
    else:
        h_ref, g_ref, b_ref = refs[:3]
        outs = refs[3:]
        x = h_ref[...]
    mu = jnp.mean(x, axis=-1, keepdims=True)
    xc = x - mu
    var = jnp.mean(xc * xc, axis=-1, keepdims=True)
    y = xc * lax.rsqrt(var + LN_EPS) * g_ref[...] + b_ref[...]
    outs[0][...] = y
    if has_bf16:
        outs[1][...] = y.astype(jnp.bfloat16)


def _layer_norm(h, res, g, b, *, alpha=1.0, want_bf16=True, tm=256):
    t, d = h.shape
    row = pl.BlockSpec((tm, d), lambda i: (i, 0))
    vec = pl.BlockSpec((1, d), lambda i: (0, 0))
    has_res = res is not None
    ins = [h] + ([res] if has_res else []) + [g.reshape(1, d), b.reshape(1, d)]
    in_specs = [row] + ([row] if has_res else []) + [vec, vec]
    out_shape = [jax.ShapeDtypeStruct((t, d), jnp.float32)]
    out_specs = [row]
    if want_bf16:
        out_shape.append(jax.ShapeDtypeStruct((t, d), jnp.bfloat16))
        out_specs.append(row)
    outs = pl.pallas_call(
        functools.partial(_ln_body, alpha=alpha, has_res=has_res, has_bf16=want_bf16),
        grid=(t // tm,), in_specs=in_specs, out_specs=out_specs, out_shape=out_shape,
        compiler_params=_params(1), name="layer_norm")(*ins)
    return outs


def _mm_body(a_ref, b_ref, o_ref, *, epilogue):
    acc = jnp.dot(a_ref[...], b_ref[...], preferred_element_type=jnp.float32)
    if epilogue == "gelu":
        acc = _gelu(acc)
    elif epilogue == "sigmoid":
        acc = jax.nn.sigmoid(acc)
    o_ref[...] = acc.astype(o_ref.dtype)


def _matmul(a, w, col0, ncols, *, epilogue=None, out_dtype=jnp.float32, tm=1024, tn=512, name="matmul"):
    m, k = a.shape
    tm = min(tm, m)
    tn = math.gcd(tn, ncols, col0) if col0 else math.gcd(tn, ncols)
    cb0 = col0 // tn
    return pl.pallas_call(
        functools.partial(_mm_body, epilogue=epilogue),
        grid=(m // tm, ncols // tn),
        in_specs=[pl.BlockSpec((tm, k), lambda i, j: (i, 0)),
                  pl.BlockSpec((k, tn), lambda i, j: (0, cb0 + j))],
        out_specs=pl.BlockSpec((tm, tn), lambda i, j: (i, j)),
        out_shape=jax.ShapeDtypeStruct((m, ncols), out_dtype),
        compiler_params=_params(2), name=name)(a, w)


def _merge_body(hr_ref, at_ref, wr_ref, wa_ref, gr_ref, ga_ref, o_ref):
    yr = jnp.dot(hr_ref[...], wr_ref[...], preferred_element_type=jnp.float32)
    ya = jnp.dot(at_ref[...], wa_ref[...], preferred_element_type=jnp.float32)
    o_ref[...] = (gr_ref[...].astype(jnp.float32) * yr + ga_ref[...].astype(jnp.float32) * ya).astype(o_ref.dtype)


def _branch_merge(hr, at, w_rnn, w_attn, gates, *, tm=1024, tn=512):
    m, kr = hr.shape
    ka = at.shape[1]
    n = w_rnn.shape[1]
    tm = min(tm, m)
    tn = min(tn, n)
    nb = n // tn
    return pl.pallas_call(
        _merge_body,
        grid=(m // tm, nb),
        in_specs=[pl.BlockSpec((tm, kr), lambda i, j: (i, 0)),
                  pl.BlockSpec((tm, ka), lambda i, j: (i, 0)),
                  pl.BlockSpec((kr, tn), lambda i, j: (0, j)),
                  pl.BlockSpec((ka, tn), lambda i, j: (0, j)),
                  pl.BlockSpec((tm, tn), lambda i, j: (i, j)),
                  pl.BlockSpec((tm, tn), lambda i, j: (i, nb + j))],
        out_specs=pl.BlockSpec((tm, tn), lambda i, j: (i, j)),
        out_shape=jax.ShapeDtypeStruct((m, n), jnp.bfloat16),
        compiler_params=_params(2), name="branch_merge")(hr, at, w_rnn, w_attn, gates, gates)


def _lru_body(x_ref, gg_ref, cw_ref, cb_ref, wg_ref, bg_ref, lam_ref, o_ref,
              xpad, af, uf, ab, ub, hs, *, seq, cblk, chunk):
    nh = cblk // LRU_BLOCK
    pad = SUBLANES
    xpad[0:pad, :] = jnp.zeros((pad, cblk), jnp.float32)
    xpad[pad + seq:2 * pad + seq, :] = jnp.zeros((pad, cblk), jnp.float32)
    xpad[pad:pad + seq, :] = x_ref[...]
    coef = -LRU_C * jax.nn.softplus(-lam_ref[...])

    def gate_chunk(c, carry):
        t0 = pl.multiple_of(c * chunk, chunk)
        win = xpad[pl.ds(t0, chunk + 2 * pad), :]
        xc = cb_ref[...]
        for kk in range(CONV_W):
            off = pad + kk - CONV_W // 2
            xc = xc + cw_ref[kk:kk + 1, :] * win[off:off + chunk, :]
        for hh in range(nh):
            lo, hi = hh * LRU_BLOCK, (hh + 1) * LRU_BLOCK
            xh = xc[:, lo:hi]
            g = jnp.dot(xh.astype(jnp.bfloat16), wg_ref[hh], preferred_element_type=jnp.float32) + bg_ref[hh]
            for d, (a_s, u_s) in enumerate(((af, uf), (ab, ub))):
                r = jax.nn.sigmoid(g[:, (2 * d) * LRU_BLOCK:(2 * d + 1) * LRU_BLOCK])
                i = jax.nn.sigmoid(g[:, (2 * d + 1) * LRU_BLOCK:(2 * d + 2) * LRU_BLOCK])
                a = jnp.exp(coef[d:d + 1, lo:hi] * r)
                u = jnp.sqrt(1.0 - a * a) * (i * xh)
                a_s[pl.ds(t0, chunk), lo:hi] = a
                u_s[pl.ds(t0, chunk), lo:hi] = u
        return carry

    lax.fori_loop(0, seq // chunk, gate_chunk, 0)

    row = lax.broadcasted_iota(jnp.int32, (SUBLANES, cblk), 0)
    ngroups = seq // SUBLANES

    def fwd_group(gi, carry):
        r0 = pl.multiple_of(gi * SUBLANES, SUBLANES)
        a = af[pl.ds(r0, SUBLANES), :]
        h = uf[pl.ds(r0, SUBLANES), :]
        for d in (1, 2, 4):
            a_sh = jnp.where(row >= d, pltpu.roll(a, d, 0), 1.0)
            h_sh = jnp.where(row >= d, pltpu.roll(h, d, 0), 0.0)
            h = a * h_sh + h
            a = a * a_sh
        h = h + a * carry
        hs[pl.ds(r0, SUBLANES), :] = h
        return h[SUBLANES - 1:SUBLANES, :]

    lax.fori_loop(0, ngroups, fwd_group, jnp.zeros((1, cblk), jnp.float32), unroll=4)

    def bwd_group(gi, carry):
        r0 = pl.multiple_of((ngroups - 1 - gi) * SUBLANES, SUBLANES)
        a = ab[pl.ds(r0, SUBLANES), :]
        h = ub[pl.ds(r0, SUBLANES), :]
        for d in (1, 2, 4):
            a_sh = jnp.where(row < SUBLANES - d, pltpu.roll(a, SUBLANES - d, 0), 1.0)
            h_sh = jnp.where(row < SUBLANES - d, pltpu.roll(h, SUBLANES - d, 0), 0.0)
            h = a * h_sh + h
            a = a * a_sh
        h = h + a * carry
        hs[pl.ds(r0, SUBLANES), :] = hs[pl.ds(r0, SUBLANES), :] + h
        return h[0:1, :]

    lax.fori_loop(0, ngroups, bwd_group, jnp.zeros((1, cblk), jnp.float32), unroll=4)

    def out_chunk(c, carry):
        t0 = pl.multiple_of(c * chunk, chunk)
        o_ref[pl.ds(t0, chunk), :] = (gg_ref[pl.ds(t0, chunk), :].astype(jnp.float32)
                                      * hs[pl.ds(t0, chunk), :]).astype(o_ref.dtype)
        return carry

    lax.fori_loop(0, seq // chunk, out_chunk, 0)


def _rg_lru(xr, gg, conv_w, conv_b, wg, bg, lam, *, batch, seq, cblk=256, chunk=256):
    t, d = xr.shape
    nh = cblk // LRU_BLOCK
    ncb = d // cblk
    tok = pl.BlockSpec((seq, cblk), lambda b, c: (b, c))
    f32 = jnp.float32
    return pl.pallas_call(
        functools.partial(_lru_body, seq=seq, cblk=cblk, chunk=chunk),
        grid=(batch, ncb),
        in_specs=[tok, tok,
                  pl.BlockSpec((CONV_W, cblk), lambda b, c: (0, c)),
                  pl.BlockSpec((1, cblk), lambda b, c: (0, c)),
                  pl.BlockSpec((nh, LRU_BLOCK, 4 * LRU_BLOCK), lambda b, c: (c, 0, 0)),
                  pl.BlockSpec((nh, 1, 4 * LRU_BLOCK), lambda b, c: (c, 0, 0)),
                  pl.BlockSpec((2, cblk), lambda b, c: (0, c))],
        out_specs=tok,
        out_shape=jax.ShapeDtypeStruct((t, d), jnp.bfloat16),
        scratch_shapes=[pltpu.VMEM((seq + 2 * SUBLANES, cblk), f32)] + [pltpu.VMEM((seq, cblk), f32)] * 5,
        compiler_params=_params(2), name="rg_lru")(xr, gg, conv_w, conv_b, wg, bg, lam)


def _attn_body(q_ref, k_ref, v_ref, bias_ref, o_ref, *, rows, scale):
    band = KH * GRID_W

    def one_row(r, carry):
        rs = jnp.clip(r - KH // 2, 0, rows - KH)
        q0 = pl.multiple_of(r * GRID_W, GRID_W)
        k0 = pl.multiple_of(rs * GRID_W, GRID_W)
        q = q_ref[pl.ds(q0, GRID_W), :]
        kb = k_ref[pl.ds(k0, band), :]
        vb = v_ref[pl.ds(k0, band), :]
        s = lax.dot_general(q, kb, (((1,), (1,)), ((), ())), preferred_element_type=jnp.float32)
        s = s * scale + bias_ref[rs - r + KH - 1]
        m = jnp.max(s, axis=-1, keepdims=True)
        p = jnp.exp(s - m)
        l = jnp.sum(p, axis=-1, keepdims=True)
        o = jnp.dot(p.astype(jnp.bfloat16), vb, preferred_element_type=jnp.float32) / l
        o_ref[pl.ds(q0, GRID_W), :] = o.astype(o_ref.dtype)
        return carry

    lax.fori_loop(0, rows, one_row, 0)


def _attn_bias_table(rpb):
    cq = np.arange(GRID_W)[:, None]
    ck = np.arange(GRID_W)[None, :]
    cs = np.clip(cq - KW // 2, 0, GRID_W - KW)
    valid = (ck >= cs) & (ck < cs + KW)
    dc = np.clip(ck - cq, -(KW - 1), KW - 1) + KW - 1
    tz = jnp.where(jnp.asarray(valid)[None, None], rpb.astype(jnp.float32)[:, :, dc], NEG_BIG)
    dr = np.arange(KH)[:, None] + np.arange(KH)[None, :]
    tb = tz[:, dr]
    h = rpb.shape[0]
    return tb.transpose(0, 1, 3, 2, 4).reshape(h, KH, GRID_W, KH * GRID_W)


def _neighbourhood_attention(qkv, rpb, *, batch, seq, n_heads):
    t = qkv.shape[0]
    rows = seq // GRID_W
    bias = _attn_bias_table(rpb)
    blk = lambda off: pl.BlockSpec((seq, HEAD_DIM), lambda b, h: (b, off + h))
    return pl.pallas_call(
        functools.partial(_attn_body, rows=rows, scale=HEAD_DIM ** -0.5),
        grid=(batch, n_heads),
        in_specs=[blk(0), blk(n_heads), blk(2 * n_heads),
                  pl.BlockSpec((None, KH, GRID_W, KH * GRID_W), lambda b, h: (h, 0, 0, 0))],
        out_specs=pl.BlockSpec((seq, HEAD_DIM), lambda b, h: (b, h)),
        out_shape=jax.ShapeDtypeStruct((t, n_heads * HEAD_DIM), jnp.bfloat16),
        compiler_params=_params(2), name="natten")(qkv, qkv, qkv, bias)


def _top16(sc, n):
    iota = lax.broadcasted_iota(jnp.int32, sc.shape, 0).astype(jnp.float32)
    vals, idxs = [], []
    for _ in range(TOPK):
        m = jnp.max(sc, axis=0, keepdims=True)
        am = jnp.min(jnp.where(sc == m, iota, float(n)), axis=0, keepdims=True)
        vals.append(m)
        idxs.append(am)
        sc = jnp.where(iota == am, -jnp.inf, sc)
    return jnp.concatenate(vals, axis=0), jnp.concatenate(idxs, axis=0)


def _pick_rows(table, sel):
    out = jnp.zeros_like(sel)
    for r in range(TOPK):
        out = jnp.where(sel == float(r), table[r:r + 1, :], out)
    return out


def _peer_topk_body(q_ref, sk_ref, i1_ref, i2_ref, g_ref):
    nt = (((1,), (1,)), ((), ()))
    sc1 = lax.dot_general(sk_ref[0], q_ref[:, 0:D_HALF], nt, preferred_element_type=jnp.float32)
    sc2 = lax.dot_general(sk_ref[1], q_ref[:, D_HALF:2 * D_HALF], nt, preferred_element_type=jnp.float32)
    v1, x1 = _top16(sc1, N_KEYS)
    v2, x2 = _top16(sc2, N_KEYS)
    cand = jnp.concatenate([v1[i:i + 1, :] + v2 for i in range(TOPK)], axis=0)
    cv, ci = _top16(cand, TOPK * TOPK)
    hi = jnp.floor(ci * (1.0 / TOPK))
    lo = ci - hi * TOPK
    i1_ref[...] = _pick_rows(x1, hi)
    i2_ref[...] = _pick_rows(x2, lo)
    e = jnp.exp(cv - cv[0:1, :])
    g_ref[...] = e / jnp.sum(e, axis=0, keepdims=True)


def _peer_topk(qp, subkeys, *, tt=256):
    t = qp.shape[0]
    out = jax.ShapeDtypeStruct((PEER_HEADS * TOPK, t), jnp.float32)
    ospec = pl.BlockSpec((TOPK, tt), lambda i, h: (h, i))
    return pl.pallas_call(
        _peer_topk_body,
        grid=(t // tt, PEER_HEADS),
        in_specs=[pl.BlockSpec((tt, 2 * D_HALF), lambda i, h: (i, h)),
                  pl.BlockSpec((None, 2, N_KEYS, D_HALF), lambda i, h: (h, 0, 0, 0))],
        out_specs=[ospec, ospec, ospec], out_shape=[out, out, out],
        compiler_params=_params(2), name="peer_topk")(qp, subkeys)


def _gate_matrix_body(i1_ref, i2_ref, g_ref, o_ref, i1s, i2s, gs, *, tt):
    i1s[...] = i1_ref[...].T
    i2s[...] = i2_ref[...].T
    gs[...] = g_ref[...].T
    ns = PEER_HEADS * TOPK
    key = lax.broadcasted_iota(jnp.int32, (N_KEYS, ns), 0).astype(jnp.float32)

    def one_token(t, carry):
        a = jnp.broadcast_to(i1s[pl.ds(t, 1), :], (N_KEYS, ns))
        b = jnp.broadcast_to(i2s[pl.ds(t, 1), :], (N_KEYS, ns))
        g = jnp.broadcast_to(gs[pl.ds(t, 1), :], (N_KEYS, ns))
        p = jnp.where(key == a, g, 0.0).astype(jnp.bfloat16)
        q = jnp.where(key == b, 1.0, 0.0).astype(jnp.bfloat16)
        o_ref[t] = lax.dot_general(p, q, (((1,), (1,)), ((), ())),
                                   preferred_element_type=jnp.float32).astype(o_ref.dtype)
        return carry

    lax.fori_loop(0, tt, one_token, 0)


def _gate_matrix(i1, i2, gate, *, tt=128):
    ns, t = i1.shape
    spec = pl.BlockSpec((ns, tt), lambda i: (0, i))
    return pl.pallas_call(
        functools.partial(_gate_matrix_body, tt=tt),
        grid=(t // tt,),
        in_specs=[spec, spec, spec],
        out_specs=pl.BlockSpec((tt, N_KEYS, N_KEYS), lambda i: (i, 0, 0)),
        out_shape=jax.ShapeDtypeStruct((t, N_KEYS, N_KEYS), jnp.bfloat16),
        scratch_shapes=[pltpu.VMEM((tt, ns), jnp.float32)] * 3,
        compiler_params=_params(1), name="peer_gate_matrix")(i1, i2, gate)


def _peer_dense_body(x_ref, u_ref, v_ref, g_ref, o_ref):
    @pl.when(pl.program_id(1) == 0)
    def _():
        o_ref[...] = jnp.zeros_like(o_ref)

    act = lax.dot_general(x_ref[...], u_ref[...], (((1,), (1,)), ((), ())), preferred_element_type=jnp.float32)
    hval = (g_ref[...].astype(jnp.float32) * _gelu(act)).astype(jnp.bfloat16)
    o_ref[...] += jnp.dot(hval, v_ref[...], preferred_element_type=jnp.float32)


def _peer_dense(x, u, v, g, *, tt=512, te=512):
    t, d = x.shape
    e = u.shape[0]
    return pl.pallas_call(
        _peer_dense_body,
        grid=(t // tt, e // te),
        in_specs=[pl.BlockSpec((tt, d), lambda i, j: (i, 0)),
                  pl.BlockSpec((te, d), lambda i, j: (j, 0)),
                  pl.BlockSpec((te, d), lambda i, j: (j, 0)),
                  pl.BlockSpec((tt, te), lambda i, j: (i, j))],
        out_specs=pl.BlockSpec((tt, d), lambda i, j: (i, 0)),
        out_shape=jax.ShapeDtypeStruct((t, d), jnp.float32),
        compiler_params=_params(2), name="peer_dense")(x, u, v, g)


def kernel(x, ln0_g, ln0_b, w_in, conv_w, conv_b, lru_wa, lru_ba, lru_wx, lru_bx, lru_lambda, rpb, w_branch_rnn, w_branch_attn, w_out, ln1_g, ln1_b, peer_wq, peer_subkeys, peer_u, peer_v, ln2_g, ln2_b):
    bf16 = jnp.bfloat16
    batch, seq, d_model = x.shape
    t = batch * seq
    d_rnn = conv_w.shape[-1]
    d_attn = w_branch_attn.shape[1]
    n_heads = d_attn // HEAD_DIM
    s1, s2 = d_rnn, 2 * d_rnn
    s5 = s2 + 3 * d_attn

    h, h16 = _layer_norm(x.reshape(t, d_model), None, ln0_g, ln0_b)
    for l in range(DEPTH):
        w_in16 = w_in[l].astype(bf16)
        xr = _matmul(h16, w_in16, 0, s1, name="w_in_x")
        gg = _matmul(h16, w_in16, s1, s2 - s1, epilogue="gelu", out_dtype=bf16, name="w_in_gate")
        qkv = _matmul(h16, w_in16, s2, s5 - s2, out_dtype=bf16, name="w_in_qkv")
        mg = _matmul(h16, w_in16, s5, 2 * d_model, epilogue="sigmoid", out_dtype=bf16, name="w_in_merge")

        wg = jnp.concatenate([lru_wa[l, 0], lru_wx[l, 0], lru_wa[l, 1], lru_wx[l, 1]], axis=-1).astype(bf16)
        lh = d_rnn // LRU_BLOCK
        bg = jnp.concatenate([lru_ba[l, 0].reshape(lh, 1, LRU_BLOCK), lru_bx[l, 0].reshape(lh, 1, LRU_BLOCK),
                              lru_ba[l, 1].reshape(lh, 1, LRU_BLOCK), lru_bx[l, 1].reshape(lh, 1, LRU_BLOCK)], axis=-1)
        hr = _rg_lru(xr, gg, conv_w[l], conv_b[l].reshape(1, d_rnn), wg, bg, lru_lambda[l], batch=batch, seq=seq)
        at = _neighbourhood_attention(qkv, rpb[l], batch=batch, seq=seq, n_heads=n_heads)

        mixed = _branch_merge(hr, at, w_branch_rnn[l].astype(bf16), w_branch_attn[l].astype(bf16), mg)
        proj = _matmul(mixed, w_out[l].astype(bf16), 0, d_model, name="w_out")
        h, h16 = _layer_norm(h, proj, ln1_g[l], ln1_b[l], alpha=ALPHA)

        qp = _matmul(h16, peer_wq[l].astype(bf16), 0, peer_wq.shape[-1], out_dtype=bf16, name="peer_query")
        i1, i2, gate = _peer_topk(qp, peer_subkeys[l].astype(bf16))
        gmat = _gate_matrix(i1, i2, gate).reshape(t, N_KEYS * N_KEYS)
        y = _peer_dense(h16, peer_u[l].astype(bf16), peer_v[l].astype(bf16), gmat)
        (h,) = _layer_norm(h, y, ln2_g[l], ln2_b[l], alpha=ALPHA, want_bf16=False)
    return h.reshape(batch, seq, d_model)
```

```python
import functools
import math

import numpy as np
import jax
import jax.numpy as jnp
from jax import lax
from jax.experimental import pallas as pl
from jax.experimental.pallas import tpu as pltpu

LRU_BLOCK = 128
LRU_C = 8.0
CONV_W = 4
HEAD_DIM = 128
GRID_W = 64
KH = 8
KW = 16
ROW_GROUP = 4
PEER_HEADS = 8
N_KEYS = 128
TOPK = 16
D_HALF = 128
LN_EPS = 1e-5
DEPTH = 1
ALPHA = (2 * DEPTH) ** 0.25

V7X_VMEM_LIMIT_BYTES = 56 * 1024 * 1024
SUBLANES = 8
LANES = 128

NEG_BIG = -1e30
SQRT_HALF = 0.7071067811865476


def _params(n_axes, vmem=V7X_VMEM_LIMIT_BYTES):
    return pltpu.CompilerParams(dimension_semantics=("arbitrary",) * n_axes, vmem_limit_bytes=vmem)


def _gelu(x):
    return 0.5 * x * (1.0 + lax.erf(x * SQRT_HALF))


def _ln_body(*refs, alpha, has_res, has_bf16):
    if has_res:
        h_ref, r_ref, g_ref, b_ref = refs[:4]
        outs = refs[4:]
        x = alpha * h_ref[...] + r_ref[...]
    else:
        h_ref, g_ref, b_ref = refs[:3]
        outs = refs[3:]
        x = h_ref[...]
    mu = jnp.mean(x, axis=-1, keepdims=True)
    xc = x - mu
    var = jnp.mean(xc * xc, axis=-1, keepdims=True)
    y = xc * lax.rsqrt(var + LN_EPS) * g_ref[...] + b_ref[...]
    outs[0][...] = y
    if has_bf16:
        outs[1][...] = y.astype(jnp.bfloat16)


def _layer_norm(h, res, g, b, *, alpha=1.0, want_bf16=True, tm=256):
    t, d = h.shape
    row = pl.BlockSpec((tm, d), lambda i: (i, 0))
    vec = pl.BlockSpec((1, d), lambda i: (0, 0))
    has_res = res is not None
    ins = [h] + ([res] if has_res else []) + [g.reshape(1, d), b.reshape(1, d)]
    in_specs = [row] + ([row] if has_res else []) + [vec, vec]
    out_shape = [jax.ShapeDtypeStruct((t, d), jnp.float32)]
    out_specs = [row]
    if want_bf16:
        out_shape.append(jax.ShapeDtypeStruct((t, d), jnp.bfloat16))
        out_specs.append(row)
    outs = pl.pallas_call(
        functools.partial(_ln_body, alpha=alpha, has_res=has_res, has_bf16=want_bf16),
        grid=(t // tm,), in_specs=in_specs, out_specs=out_specs, out_shape=out_shape,
        compiler_params=_params(1), name="layer_norm")(*ins)
    return outs


def _mm_body(a_ref, b_ref, o_ref, *, epilogue):
    acc = jnp.dot(a_ref[...], b_ref[...], preferred_element_type=jnp.float32)
    if epilogue == "gelu":
        acc = _gelu(acc)
    elif epilogue == "sigmoid":
        acc = jax.nn.sigmoid(acc)
    o_ref[...] = acc.astype(o_ref.dtype)


def _matmul(a, w, col0, ncols, *, epilogue=None, out_dtype=jnp.float32, tm=1024, tn=512, name="matmul"):
    m, k = a.shape
    tm = min(tm, m)
    tn = math.gcd(tn, ncols, col0) if col0 else math.gcd(tn, ncols)
    cb0 = col0 // tn
    return pl.pallas_call(
        functools.partial(_mm_body, epilogue=epilogue),
        grid=(m // tm, ncols // tn),
        in_specs=[pl.BlockSpec((tm, k), lambda i, j: (i, 0)),
                  pl.BlockSpec((k, tn), lambda i, j: (0, cb0 + j))],
        out_specs=pl.BlockSpec((tm, tn), lambda i, j: (i, j)),
        out_shape=jax.ShapeDtypeStruct((m, ncols), out_dtype),
        compiler_params=_params(2), name=name)(a, w)


def _merge_body(hr_ref, at_ref, wr_ref, wa_ref, gr_ref, ga_ref, o_ref):
    yr = jnp.dot(hr_ref[...], wr_ref[...], preferred_element_type=jnp.float32)
    ya = jnp.dot(at_ref[...], wa_ref[...], preferred_element_type=jnp.float32)
    o_ref[...] = (gr_ref[...].astype(jnp.float32) * yr + ga_ref[...].astype(jnp.float32) * ya).astype(o_ref.dtype)


def _branch_merge(hr, at, w_rnn, w_attn, gates, *, tm=1024, tn=512):
    m, kr = hr.shape
    ka = at.shape[1]
    n = w_rnn.shape[1]
    tm = min(tm, m)
    tn = min(tn, n)
    nb = n // tn
    return pl.pallas_call(
        _merge_body,
        grid=(m // tm, nb),
        in_specs=[pl.BlockSpec((tm, kr), lambda i, j: (i, 0)),
                  pl.BlockSpec((tm, ka), lambda i, j: (i, 0)),
                  pl.BlockSpec((kr, tn), lambda i, j: (0, j)),
                  pl.BlockSpec((ka, tn), lambda i, j: (0, j)),
                  pl.BlockSpec((tm, tn), lambda i, j: (i, j)),
                  pl.BlockSpec((tm, tn), lambda i, j: (i, nb + j))],
        out_specs=pl.BlockSpec((tm, tn), lambda i, j: (i, j)),
        out_shape=jax.ShapeDtypeStruct((m, n), jnp.bfloat16),
        compiler_params=_params(2), name="branch_merge")(hr, at, w_rnn, w_attn, gates, gates)


def _lru_body(x_ref, gg_ref, cw_ref, cb_ref, wg_ref, bg_ref, lam_ref, o_ref,
              xpad, af, uf, ab, ub, hs, *, seq, cblk, chunk):
    nh = cblk // LRU_BLOCK
    pad = SUBLANES
    xpad[0:pad, :] = jnp.zeros((pad, cblk), jnp.float32)
    xpad[pad + seq:2 * pad + seq, :] = jnp.zeros((pad, cblk), jnp.float32)
    xpad[pad:pad + seq, :] = x_ref[...]
    coef = -LRU_C * jax.nn.softplus(-lam_ref[...])

    def gate_chunk(c, carry):
        t0 = pl.multiple_of(c * chunk, chunk)
        win = xpad[pl.ds(t0, chunk + 2 * pad), :]
        xc = cb_ref[...]
        for kk in range(CONV_W):
            off = pad + kk - CONV_W // 2
            xc = xc + cw_ref[kk:kk + 1, :] * win[off:off + chunk, :]
        for hh in range(nh):
            lo, hi = hh * LRU_BLOCK, (hh + 1) * LRU_BLOCK
            xh = xc[:, lo:hi]
            g = jnp.dot(xh.astype(jnp.bfloat16), wg_ref[hh], preferred_element_type=jnp.float32) + bg_ref[hh]
            for d, (a_s, u_s) in enumerate(((af, uf), (ab, ub))):
                r = jax.nn.sigmoid(g[:, (2 * d) * LRU_BLOCK:(2 * d + 1) * LRU_BLOCK])
                i = jax.nn.sigmoid(g[:, (2 * d + 1) * LRU_BLOCK:(2 * d + 2) * LRU_BLOCK])
                a = jnp.exp(coef[d:d + 1, lo:hi] * r)
                u = jnp.sqrt(1.0 - a * a) * (i * xh)
                a_s[pl.ds(t0, chunk), lo:hi] = a
                u_s[pl.ds(t0, chunk), lo:hi] = u
        return carry

    lax.fori_loop(0, seq // chunk, gate_chunk, 0)

    row = lax.broadcasted_iota(jnp.int32, (SUBLANES, cblk), 0)
    ngroups = seq // SUBLANES

    def fwd_group(gi, carry):
        r0 = pl.multiple_of(gi * SUBLANES, SUBLANES)
        a = af[pl.ds(r0, SUBLANES), :]
        h = uf[pl.ds(r0, SUBLANES), :]
        for d in (1, 2, 4):
            a_sh = jnp.where(row >= d, pltpu.roll(a, d, 0), 1.0)
            h_sh = jnp.where(row >= d, pltpu.roll(h, d, 0), 0.0)
            h = a * h_sh + h
            a = a * a_sh
        h = h + a * carry
        hs[pl.ds(r0, SUBLANES), :] = h
        return h[SUBLANES - 1:SUBLANES, :]

    lax.fori_loop(0, ngroups, fwd_group, jnp.zeros((1, cblk), jnp.float32), unroll=4)

    def bwd_group(gi, carry):
        r0 = pl.multiple_of((ngroups - 1 - gi) * SUBLANES, SUBLANES)
        a = ab[pl.ds(r0, SUBLANES), :]
        h = ub[pl.ds(r0, SUBLANES), :]
        for d in (1, 2, 4):
            a_sh = jnp.where(row < SUBLANES - d, pltpu.roll(a, SUBLANES - d, 0), 1.0)
            h_sh = jnp.where(row < SUBLANES - d, pltpu.roll(h, SUBLANES - d, 0), 0.0)
            h = a * h_sh + h
            a = a * a_sh
        h = h + a * carry
        hs[pl.ds(r0, SUBLANES), :] = hs[pl.ds(r0, SUBLANES), :] + h
        return h[0:1, :]

    lax.fori_loop(0, ngroups, bwd_group, jnp.zeros((1, cblk), jnp.float32), unroll=4)

    def out_chunk(c, carry):
        t0 = pl.multiple_of(c * chunk, chunk)
        o_ref[pl.ds(t0, chunk), :] = (gg_ref[pl.ds(t0, chunk), :].astype(jnp.float32)
                                      * hs[pl.ds(t0, chunk), :]).astype(o_ref.dtype)
        return carry

    lax.fori_loop(0, seq // chunk, out_chunk, 0)


def _rg_lru(xr, gg, conv_w, conv_b, wg, bg, lam, *, batch, seq, cblk=256, chunk=256):
    t, d = xr.shape
    nh = cblk // LRU_BLOCK
    ncb = d // cblk
    tok = pl.BlockSpec((seq, cblk), lambda b, c: (b, c))
    f32 = jnp.float32
    return pl.pallas_call(
        functools.partial(_lru_body, seq=seq, cblk=cblk, chunk=chunk),
        grid=(batch, ncb),
        in_specs=[tok, tok,
                  pl.BlockSpec((CONV_W, cblk), lambda b, c: (0, c)),
                  pl.BlockSpec((1, cblk), lambda b, c: (0, c)),
                  pl.BlockSpec((nh, LRU_BLOCK, 4 * LRU_BLOCK), lambda b, c: (c, 0, 0)),
                  pl.BlockSpec((nh, 1, 4 * LRU_BLOCK), lambda b, c: (c, 0, 0)),
                  pl.BlockSpec((2, cblk), lambda b, c: (0, c))],
        out_specs=tok,
        out_shape=jax.ShapeDtypeStruct((t, d), jnp.bfloat16),
        scratch_shapes=[pltpu.VMEM((seq + 2 * SUBLANES, cblk), f32)] + [pltpu.VMEM((seq, cblk), f32)] * 5,
        compiler_params=_params(2), name="rg_lru")(xr, gg, conv_w, conv_b, wg, bg, lam)


def _attn_body(q_ref, k_ref, v_ref, bias_ref, o_ref, *, rows, scale):
    band = KH * GRID_W

    def row_group(rg, carry):
        pos, scores, probs = [], [], []
        for u in range(ROW_GROUP):
            r = rg * ROW_GROUP + u
            rs = jnp.clip(r - KH // 2, 0, rows - KH)
            q0 = pl.multiple_of(r * GRID_W, GRID_W)
            k0 = pl.multiple_of(rs * GRID_W, GRID_W)
            pos.append((q0, k0))
            s = lax.dot_general(q_ref[pl.ds(q0, GRID_W), :], k_ref[pl.ds(k0, band), :],
                                (((1,), (1,)), ((), ())), preferred_element_type=jnp.float32)
            scores.append(s * scale + bias_ref[rs - r + KH - 1])
        for s in scores:
            p = jnp.exp(s - jnp.max(s, axis=-1, keepdims=True))
            probs.append((p.astype(jnp.bfloat16), jnp.sum(p, axis=-1, keepdims=True)))
        for (q0, k0), (p, l) in zip(pos, probs):
            o = jnp.dot(p, v_ref[pl.ds(k0, band), :], preferred_element_type=jnp.float32) / l
            o_ref[pl.ds(q0, GRID_W), :] = o.astype(o_ref.dtype)
        return carry

    lax.fori_loop(0, rows // ROW_GROUP, row_group, 0)


def _attn_bias_table(rpb):
    cq = np.arange(GRID_W)[:, None]
    ck = np.arange(GRID_W)[None, :]
    cs = np.clip(cq - KW // 2, 0, GRID_W - KW)
    valid = (ck >= cs) & (ck < cs + KW)
    dc = np.clip(ck - cq, -(KW - 1), KW - 1) + KW - 1
    tz = jnp.where(jnp.asarray(valid)[None, None], rpb.astype(jnp.float32)[:, :, dc], NEG_BIG)
    dr = np.arange(KH)[:, None] + np.arange(KH)[None, :]
    tb = tz[:, dr]
    h = rpb.shape[0]
    return tb.transpose(0, 1, 3, 2, 4).reshape(h, KH, GRID_W, KH * GRID_W)


def _neighbourhood_attention(qkv, rpb, *, batch, seq, n_heads):
    t = qkv.shape[0]
    rows = seq // GRID_W
    bias = _attn_bias_table(rpb)
    blk = lambda off: pl.BlockSpec((seq, HEAD_DIM), lambda b, h: (b, off + h))
    return pl.pallas_call(
        functools.partial(_attn_body, rows=rows, scale=HEAD_DIM ** -0.5),
        grid=(batch, n_heads),
        in_specs=[blk(0), blk(n_heads), blk(2 * n_heads),
                  pl.BlockSpec((None, KH, GRID_W, KH * GRID_W), lambda b, h: (h, 0, 0, 0))],
        out_specs=pl.BlockSpec((seq, HEAD_DIM), lambda b, h: (b, h)),
        out_shape=jax.ShapeDtypeStruct((t, n_heads * HEAD_DIM), jnp.bfloat16),
        compiler_params=_params(2), name="natten")(qkv, qkv, qkv, bias)


def _top16(sc, ids):
    vals, idxs = [], []
    for _ in range(TOPK):
        m = jnp.max(sc, axis=0, keepdims=True)
        am = jnp.min(jnp.where(sc == m, ids, float(TOPK * TOPK)), axis=0, keepdims=True)
        vals.append(m)
        idxs.append(am)
        sc = jnp.where(ids == am, -jnp.inf, sc)
    return jnp.concatenate(vals, axis=0), jnp.concatenate(idxs, axis=0)


def _stair_rows():
    rows = [(0, j, True) for j in range(8)] + [(0, j, True) for j in range(8, 16)]
    rows += [(1, j, True) for j in range(8)]
    rows += [(i, 0, True) for i in range(8, 16)]
    rows += [(i, 1, i >= 2) for i in range(8)]
    rows += [(i, 0, i >= 2) for i in range(8)]
    rows += [(2, 2, True), (2, 3, True), (2, 4, True), (3, 2, True), (3, 3, True), (4, 2, True),
             (0, 0, False), (0, 1, False)]
    want = {(i, j) for i in range(TOPK) for j in range(TOPK) if (i + 1) * (j + 1) <= TOPK}
    got = [(i, j) for i, j, ok in rows if ok]
    assert len(got) == len(set(got)) and set(got) == want
    flat = np.array([i * TOPK + j for i, j, _ in rows], np.float32)
    mask = np.array([0.0 if ok else -np.inf for _, _, ok in rows], np.float32)
    return flat, mask


def _stair_candidates(v1, v2):
    tail = jnp.concatenate([v1[2:3] + v2[2:5], v1[3:4] + v2[2:4], v1[4:5] + v2[2:3], v1[0:1] + v2[0:2]], axis=0)
    return jnp.concatenate([v1[0:1] + v2[0:8], v1[0:1] + v2[8:16], v1[1:2] + v2[0:8], v1[8:16] + v2[0:1],
                            v1[0:8] + v2[1:2], v1[0:8] + v2[0:1], tail], axis=0)


def _pick_rows(table, sel):
    out = jnp.zeros_like(sel)
    for r in range(TOPK):
        out = jnp.where(sel == float(r), table[r:r + 1, :], out)
    return out


def _peer_topk_body(q_ref, sk_ref, flat_ref, mask_ref, i1_ref, i2_ref, g_ref, *, tt):
    nt = (((1,), (1,)), ((), ()))
    sc1 = lax.dot_general(sk_ref[0], q_ref[:, 0:D_HALF], nt, preferred_element_type=jnp.float32)
    sc2 = lax.dot_general(sk_ref[1], q_ref[:, D_HALF:2 * D_HALF], nt, preferred_element_type=jnp.float32)
    key_ids = lax.broadcasted_iota(jnp.int32, sc1.shape, 0).astype(jnp.float32)
    v1, x1 = _top16(sc1, key_ids)
    v2, x2 = _top16(sc2, key_ids)
    rep = tt // LANES
    flat = jnp.concatenate([flat_ref[...]] * rep, axis=1)
    cand = _stair_candidates(v1, v2) + jnp.concatenate([mask_ref[...]] * rep, axis=1)
    cv, ci = _top16(cand, flat)
    hi = jnp.floor(ci * (1.0 / TOPK))
    lo = ci - hi * TOPK
    i1_ref[...] = _pick_rows(x1, hi)
    i2_ref[...] = _pick_rows(x2, lo)
    e = jnp.exp(cv - cv[0:1, :])
    g_ref[...] = e / jnp.sum(e, axis=0, keepdims=True)


def _peer_topk(qp, subkeys, *, tt=512):
    t = qp.shape[0]
    flat, mask = _stair_rows()
    nrows = flat.shape[0]
    flat = jnp.asarray(np.tile(flat[:, None], (1, LANES)))
    mask = jnp.asarray(np.tile(mask[:, None], (1, LANES)))
    out = jax.ShapeDtypeStruct((PEER_HEADS * TOPK, t), jnp.float32)
    ospec = pl.BlockSpec((TOPK, tt), lambda i, h: (h, i))
    const = pl.BlockSpec((nrows, LANES), lambda i, h: (0, 0))
    return pl.pallas_call(
        functools.partial(_peer_topk_body, tt=tt),
        grid=(t // tt, PEER_HEADS),
        in_specs=[pl.BlockSpec((tt, 2 * D_HALF), lambda i, h: (i, h)),
                  pl.BlockSpec((None, 2, N_KEYS, D_HALF), lambda i, h: (h, 0, 0, 0)),
                  const, const],
        out_specs=[ospec, ospec, ospec], out_shape=[out, out, out],
        compiler_params=_params(2), name="peer_topk")(qp, subkeys, flat, mask)


G_PITCH = N_KEYS + SUBLANES


def _gate_matrix_body(i1_ref, i2_ref, g_ref, o_ref, i1s, i2s, gs, gpair, *, tt):
    i1s[...] = i1_ref[...].T
    i2s[...] = i2_ref[...].T
    gs[...] = g_ref[...].T
    ns = PEER_HEADS * TOPK
    key = lax.broadcasted_iota(jnp.int32, (N_KEYS, ns), 0).astype(jnp.float32)

    def token_gate(t):
        a = jnp.broadcast_to(i1s[pl.ds(t, 1), :], (N_KEYS, ns))
        b = jnp.broadcast_to(i2s[pl.ds(t, 1), :], (N_KEYS, ns))
        g = jnp.broadcast_to(gs[pl.ds(t, 1), :], (N_KEYS, ns))
        p = jnp.where(key == a, g, 0.0).astype(jnp.bfloat16)
        q = jnp.where(key == b, 1.0, 0.0).astype(jnp.bfloat16)
        return lax.dot_general(p, q, (((1,), (1,)), ((), ())), preferred_element_type=jnp.float32)

    def one_pair(pi, carry):
        lo = lax.bitcast_convert_type(token_gate(2 * pi).astype(jnp.bfloat16).astype(jnp.float32), jnp.uint32)
        hi = lax.bitcast_convert_type(token_gate(2 * pi + 1).astype(jnp.bfloat16).astype(jnp.float32), jnp.uint32)
        packed = (hi & jnp.uint32(0xFFFF0000)) | (lo >> 16)
        gpair[pl.ds(pl.multiple_of(pi * G_PITCH, SUBLANES), N_KEYS), :] = packed
        return carry

    lax.fori_loop(0, tt // 2, one_pair, 0, unroll=16)

    def one_key(a, carry):
        o_ref[a] = gpair[pl.ds(a, tt // 2, stride=G_PITCH), :]
        return carry

    lax.fori_loop(0, N_KEYS, one_key, 0, unroll=2)


def _gate_matrix(i1, i2, gate, *, tt=256):
    ns, t = i1.shape
    spec = pl.BlockSpec((ns, tt), lambda i: (0, i))
    return pl.pallas_call(
        functools.partial(_gate_matrix_body, tt=tt),
        grid=(t // tt,),
        in_specs=[spec, spec, spec],
        out_specs=pl.BlockSpec((N_KEYS, tt // 2, N_KEYS), lambda i: (0, i, 0)),
        out_shape=jax.ShapeDtypeStruct((N_KEYS, t // 2, N_KEYS), jnp.uint32),
        scratch_shapes=[pltpu.VMEM((tt, ns), jnp.float32)] * 3
        + [pltpu.VMEM((tt // 2 * G_PITCH, N_KEYS), jnp.uint32)],
        compiler_params=_params(1), name="peer_gate_matrix")(i1, i2, gate)


def _peer_dense_body(x_ref, u_ref, v_ref, g_ref, o_ref, *, na):
    @pl.when(pl.program_id(1) == 0)
    def _():
        o_ref[...] = jnp.zeros_like(o_ref)

    act = lax.dot_general(x_ref[...], u_ref[...], (((1,), (1,)), ((), ())), preferred_element_type=jnp.float32)
    g = jnp.concatenate([pltpu.bitcast(g_ref[k], jnp.bfloat16) for k in range(na)], axis=1)
    hval = (g.astype(jnp.float32) * _gelu(act)).astype(jnp.bfloat16)
    o_ref[...] += jnp.dot(hval, v_ref[...], preferred_element_type=jnp.float32)


def _peer_dense(x, u, v, g, *, tt=512, te=512):
    t, d = x.shape
    e = u.shape[0]
    na = te // N_KEYS
    return pl.pallas_call(
        functools.partial(_peer_dense_body, na=na),
        grid=(t // tt, e // te),
        in_specs=[pl.BlockSpec((tt, d), lambda i, j: (i, 0)),
                  pl.BlockSpec((te, d), lambda i, j: (j, 0)),
                  pl.BlockSpec((te, d), lambda i, j: (j, 0)),
                  pl.BlockSpec((na, tt // 2, N_KEYS), lambda i, j: (j, i, 0))],
        out_specs=pl.BlockSpec((tt, d), lambda i, j: (i, 0)),
        out_shape=jax.ShapeDtypeStruct((t, d), jnp.float32),
        compiler_params=_params(2), name="peer_dense")(x, u, v, g)


def kernel(x, ln0_g, ln0_b, w_in, conv_w, conv_b, lru_wa, lru_ba, lru_wx, lru_bx, lru_lambda, rpb, w_branch_rnn, w_branch_attn, w_out, ln1_g, ln1_b, peer_wq, peer_subkeys, peer_u, peer_v, ln2_g, ln2_b):
    bf16 = jnp.bfloat16
    batch, seq, d_model = x.shape
    t = batch * seq
    d_rnn = conv_w.shape[-1]
    d_attn = w_branch_attn.shape[1]
    n_heads = d_attn // HEAD_DIM
    s1, s2 = d_rnn, 2 * d_rnn
    s5 = s2 + 3 * d_attn

    h, h16 = _layer_norm(x.reshape(t, d_model), None, ln0_g, ln0_b)
    for l in range(DEPTH):
        w_in16 = w_in[l].astype(bf16)
        xr = _matmul(h16, w_in16, 0, s1, name="w_in_x")
        gg = _matmul(h16, w_in16, s1, s2 - s1, epilogue="gelu", out_dtype=bf16, name="w_in_gate")
        qkv = _matmul(h16, w_in16, s2, s5 - s2, out_dtype=bf16, name="w_in_qkv")
        mg = _matmul(h16, w_in16, s5, 2 * d_model, epilogue="sigmoid", out_dtype=bf16, name="w_in_merge")

        wg = jnp.concatenate([lru_wa[l, 0], lru_wx[l, 0], lru_wa[l, 1], lru_wx[l, 1]], axis=-1).astype(bf16)
        lh = d_rnn // LRU_BLOCK
        bg = jnp.concatenate([lru_ba[l, 0].reshape(lh, 1, LRU_BLOCK), lru_bx[l, 0].reshape(lh, 1, LRU_BLOCK),
                              lru_ba[l, 1].reshape(lh, 1, LRU_BLOCK), lru_bx[l, 1].reshape(lh, 1, LRU_BLOCK)], axis=-1)
        hr = _rg_lru(xr, gg, conv_w[l], conv_b[l].reshape(1, d_rnn), wg, bg, lru_lambda[l], batch=batch, seq=seq)
        at = _neighbourhood_attention(qkv, rpb[l], batch=batch, seq=seq, n_heads=n_heads)

        mixed = _branch_merge(hr, at, w_branch_rnn[l].astype(bf16), w_branch_attn[l].astype(bf16), mg)
        proj = _matmul(mixed, w_out[l].astype(bf16), 0, d_model, name="w_out")
        h, h16 = _layer_norm(h, proj, ln1_g[l], ln1_b[l], alpha=ALPHA)

        qp = _matmul(h16, peer_wq[l].astype(bf16), 0, peer_wq.shape[-1], out_dtype=bf16, name="peer_query")
        i1, i2, gate = _peer_topk(qp, peer_subkeys[l].astype(bf16))
        gmat = _gate_matrix(i1, i2, gate)
        y = _peer_dense(h16, peer_u[l].astype(bf16), peer_v[l].astype(bf16), gmat)
        (h,) = _layer_norm(h, y, ln2_g[l], ln2_b[l], alpha=ALPHA, want_bf16=False)
    return h.reshape(batch, seq, d_model)
```

```python
import functools
import math

import numpy as np
import jax
import jax.numpy as jnp
from jax import lax
from jax.experimental import pallas as pl
from jax.experimental.pallas import tpu as pltpu

LRU_BLOCK = 128
LRU_C = 8.0
CONV_W = 4
HEAD_DIM = 128
GRID_W = 64
KH = 8
KW = 16
ROW_GROUP = 8
PEER_HEADS = 8
N_KEYS = 128
TOPK = 16
D_HALF = 128
LN_EPS = 1e-5
DEPTH = 1
ALPHA = (2 * DEPTH) ** 0.25

V7X_VMEM_LIMIT_BYTES = 56 * 1024 * 1024
SUBLANES = 8
LANES = 128

NEG_BIG = -1e30
SQRT_HALF = 0.7071067811865476


def _params(n_axes, vmem=V7X_VMEM_LIMIT_BYTES):
    return pltpu.CompilerParams(dimension_semantics=("arbitrary",) * n_axes, vmem_limit_bytes=vmem)


def _gelu(x):
    return 0.5 * x * (1.0 + lax.erf(x * SQRT_HALF))


def _sigmoid(x):
    return 0.5 * jnp.tanh(0.5 * x) + 0.5


def _ln_body(*refs, alpha, has_res, has_bf16):
    if has_res:
        h_ref, r_ref, g_ref, b_ref = refs[:4]
        outs = refs[4:]
        x = alpha * h_ref[...] + r_ref[...]
    else:
        h_ref, g_ref, b_ref = refs[:3]
        outs = refs[3:]
        x = h_ref[...]
    mu = jnp.mean(x, axis=-1, keepdims=True)
    xc = x - mu
    var = jnp.mean(xc * xc, axis=-1, keepdims=True)
    y = xc * lax.rsqrt(var + LN_EPS) * g_ref[...] + b_ref[...]
    outs[0][...] = y
    if has_bf16:
        outs[1][...] = y.astype(jnp.bfloat16)


def _layer_norm(h, res, g, b, *, alpha=1.0, want_bf16=True, tm=256):
    t, d = h.shape
    row = pl.BlockSpec((tm, d), lambda i: (i, 0))
    vec = pl.BlockSpec((1, d), lambda i: (0, 0))
    has_res = res is not None
    ins = [h] + ([res] if has_res else []) + [g.reshape(1, d), b.reshape(1, d)]
    in_specs = [row] + ([row] if has_res else []) + [vec, vec]
    out_shape = [jax.ShapeDtypeStruct((t, d), jnp.float32)]
    out_specs = [row]
    if want_bf16:
        out_shape.append(jax.ShapeDtypeStruct((t, d), jnp.bfloat16))
        out_specs.append(row)
    outs = pl.pallas_call(
        functools.partial(_ln_body, alpha=alpha, has_res=has_res, has_bf16=want_bf16),
        grid=(t // tm,), in_specs=in_specs, out_specs=out_specs, out_shape=out_shape,
        compiler_params=_params(1), name="layer_norm")(*ins)
    return outs


def _mm_body(a_ref, b_ref, o_ref, *, epilogue):
    acc = jnp.dot(a_ref[...], b_ref[...], preferred_element_type=jnp.float32)
    if epilogue == "gelu":
        acc = _gelu(acc)
    elif epilogue == "sigmoid":
        acc = _sigmoid(acc)
    o_ref[...] = acc.astype(o_ref.dtype)


def _matmul(a, w, col0, ncols, *, epilogue=None, out_dtype=jnp.float32, tm=1024, tn=1024, name="matmul"):
    m, k = a.shape
    tm = min(tm, m)
    tn = math.gcd(tn, ncols, col0) if col0 else math.gcd(tn, ncols)
    cb0 = col0 // tn
    return pl.pallas_call(
        functools.partial(_mm_body, epilogue=epilogue),
        grid=(m // tm, ncols // tn),
        in_specs=[pl.BlockSpec((tm, k), lambda i, j: (i, 0)),
                  pl.BlockSpec((k, tn), lambda i, j: (0, cb0 + j))],
        out_specs=pl.BlockSpec((tm, tn), lambda i, j: (i, j)),
        out_shape=jax.ShapeDtypeStruct((m, ncols), out_dtype),
        compiler_params=_params(2), name=name)(a, w)


def _merge_body(hr_ref, at_ref, wr_ref, wa_ref, gr_ref, ga_ref, o_ref):
    yr = jnp.dot(hr_ref[...], wr_ref[...], preferred_element_type=jnp.float32)
    ya = jnp.dot(at_ref[...], wa_ref[...], preferred_element_type=jnp.float32)
    o_ref[...] = (gr_ref[...].astype(jnp.float32) * yr + ga_ref[...].astype(jnp.float32) * ya).astype(o_ref.dtype)


def _branch_merge(hr, at, w_rnn, w_attn, gates, *, tm=1024, tn=512):
    m, kr = hr.shape
    ka = at.shape[1]
    n = w_rnn.shape[1]
    tm = min(tm, m)
    tn = min(tn, n)
    nb = n // tn
    return pl.pallas_call(
        _merge_body,
        grid=(m // tm, nb),
        in_specs=[pl.BlockSpec((tm, kr), lambda i, j: (i, 0)),
                  pl.BlockSpec((tm, ka), lambda i, j: (i, 0)),
                  pl.BlockSpec((kr, tn), lambda i, j: (0, j)),
                  pl.BlockSpec((ka, tn), lambda i, j: (0, j)),
                  pl.BlockSpec((tm, tn), lambda i, j: (i, j)),
                  pl.BlockSpec((tm, tn), lambda i, j: (i, nb + j))],
        out_specs=pl.BlockSpec((tm, tn), lambda i, j: (i, j)),
        out_shape=jax.ShapeDtypeStruct((m, n), jnp.bfloat16),
        compiler_params=_params(2), name="branch_merge")(hr, at, w_rnn, w_attn, gates, gates)


def _lru_body(x_ref, gg_ref, cw_ref, cb_ref, wg_ref, bg_ref, lam_ref, o_ref,
              xpad, af, uf, ab, ub, hs, *, seq, cblk, chunk):
    nh = cblk // LRU_BLOCK
    pad = SUBLANES
    xpad[0:pad, :] = jnp.zeros((pad, cblk), jnp.float32)
    xpad[pad + seq:2 * pad + seq, :] = jnp.zeros((pad, cblk), jnp.float32)
    xpad[pad:pad + seq, :] = x_ref[...]
    coef = -LRU_C * jax.nn.softplus(-lam_ref[...])

    def gate_chunk(c, carry):
        t0 = pl.multiple_of(c * chunk, chunk)
        win = xpad[pl.ds(t0, chunk + 2 * pad), :]
        xc = cb_ref[...]
        for kk in range(CONV_W):
            off = pad + kk - CONV_W // 2
            xc = xc + cw_ref[kk:kk + 1, :] * win[off:off + chunk, :]
        for hh in range(nh):
            lo, hi = hh * LRU_BLOCK, (hh + 1) * LRU_BLOCK
            xh = xc[:, lo:hi]
            g = jnp.dot(xh.astype(jnp.bfloat16), wg_ref[hh], preferred_element_type=jnp.float32) + bg_ref[hh]
            for d, (a_s, u_s) in enumerate(((af, uf), (ab, ub))):
                r = _sigmoid(g[:, (2 * d) * LRU_BLOCK:(2 * d + 1) * LRU_BLOCK])
                i = _sigmoid(g[:, (2 * d + 1) * LRU_BLOCK:(2 * d + 2) * LRU_BLOCK])
                a = jnp.exp(coef[d:d + 1, lo:hi] * r)
                u = jnp.sqrt(1.0 - a * a) * (i * xh)
                a_s[pl.ds(t0, chunk), lo:hi] = a
                u_s[pl.ds(t0, chunk), lo:hi] = u
        return carry

    lax.fori_loop(0, seq // chunk, gate_chunk, 0)

    row = lax.broadcasted_iota(jnp.int32, (SUBLANES, cblk), 0)
    ngroups = seq // SUBLANES

    def fwd_group(gi, carry):
        r0 = pl.multiple_of(gi * SUBLANES, SUBLANES)
        a = af[pl.ds(r0, SUBLANES), :]
        h = uf[pl.ds(r0, SUBLANES), :]
        for d in (1, 2, 4):
            a_sh = jnp.where(row >= d, pltpu.roll(a, d, 0), 1.0)
            h_sh = jnp.where(row >= d, pltpu.roll(h, d, 0), 0.0)
            h = a * h_sh + h
            a = a * a_sh
        h = h + a * carry
        hs[pl.ds(r0, SUBLANES), :] = h
        return h[SUBLANES - 1:SUBLANES, :]

    lax.fori_loop(0, ngroups, fwd_group, jnp.zeros((1, cblk), jnp.float32), unroll=4)

    def bwd_group(gi, carry):
        r0 = pl.multiple_of((ngroups - 1 - gi) * SUBLANES, SUBLANES)
        a = ab[pl.ds(r0, SUBLANES), :]
        h = ub[pl.ds(r0, SUBLANES), :]
        for d in (1, 2, 4):
            a_sh = jnp.where(row < SUBLANES - d, pltpu.roll(a, SUBLANES - d, 0), 1.0)
            h_sh = jnp.where(row < SUBLANES - d, pltpu.roll(h, SUBLANES - d, 0), 0.0)
            h = a * h_sh + h
            a = a * a_sh
        h = h + a * carry
        hs[pl.ds(r0, SUBLANES), :] = hs[pl.ds(r0, SUBLANES), :] + h
        return h[0:1, :]

    lax.fori_loop(0, ngroups, bwd_group, jnp.zeros((1, cblk), jnp.float32), unroll=4)

    def out_chunk(c, carry):
        t0 = pl.multiple_of(c * chunk, chunk)
        o_ref[pl.ds(t0, chunk), :] = (gg_ref[pl.ds(t0, chunk), :].astype(jnp.float32)
                                      * hs[pl.ds(t0, chunk), :]).astype(o_ref.dtype)
        return carry

    lax.fori_loop(0, seq // chunk, out_chunk, 0)


def _rg_lru(xr, gg, conv_w, conv_b, wg, bg, lam, *, batch, seq, cblk=256, chunk=256):
    t, d = xr.shape
    nh = cblk // LRU_BLOCK
    ncb = d // cblk
    tok = pl.BlockSpec((seq, cblk), lambda b, c: (b, c))
    f32 = jnp.float32
    return pl.pallas_call(
        functools.partial(_lru_body, seq=seq, cblk=cblk, chunk=chunk),
        grid=(batch, ncb),
        in_specs=[tok, tok,
                  pl.BlockSpec((CONV_W, cblk), lambda b, c: (0, c)),
                  pl.BlockSpec((1, cblk), lambda b, c: (0, c)),
                  pl.BlockSpec((nh, LRU_BLOCK, 4 * LRU_BLOCK), lambda b, c: (c, 0, 0)),
                  pl.BlockSpec((nh, 1, 4 * LRU_BLOCK), lambda b, c: (c, 0, 0)),
                  pl.BlockSpec((2, cblk), lambda b, c: (0, c))],
        out_specs=tok,
        out_shape=jax.ShapeDtypeStruct((t, d), jnp.bfloat16),
        scratch_shapes=[pltpu.VMEM((seq + 2 * SUBLANES, cblk), f32)] + [pltpu.VMEM((seq, cblk), f32)] * 5,
        compiler_params=_params(2), name="rg_lru")(xr, gg, conv_w, conv_b, wg, bg, lam)


def _attn_body(q_ref, k_ref, v_ref, bias_ref, o_ref, *, rows, scale):
    band = KH * GRID_W

    def row_group(rg, carry):
        pos, scores, probs = [], [], []
        for u in range(ROW_GROUP):
            r = rg * ROW_GROUP + u
            rs = jnp.clip(r - KH // 2, 0, rows - KH)
            q0 = pl.multiple_of(r * GRID_W, GRID_W)
            k0 = pl.multiple_of(rs * GRID_W, GRID_W)
            pos.append((q0, k0))
            s = lax.dot_general(q_ref[pl.ds(q0, GRID_W), :], k_ref[pl.ds(k0, band), :],
                                (((1,), (1,)), ((), ())), preferred_element_type=jnp.float32)
            scores.append(s * scale + bias_ref[rs - r + KH - 1])
        for s in scores:
            p = jnp.exp(s - jnp.max(s, axis=-1, keepdims=True))
            probs.append((p.astype(jnp.bfloat16), jnp.sum(p, axis=-1, keepdims=True)))
        for (q0, k0), (p, l) in zip(pos, probs):
            o = jnp.dot(p, v_ref[pl.ds(k0, band), :], preferred_element_type=jnp.float32) / l
            o_ref[pl.ds(q0, GRID_W), :] = o.astype(o_ref.dtype)
        return carry

    lax.fori_loop(0, rows // ROW_GROUP, row_group, 0)


def _attn_bias_table(rpb):
    cq = np.arange(GRID_W)[:, None]
    ck = np.arange(GRID_W)[None, :]
    cs = np.clip(cq - KW // 2, 0, GRID_W - KW)
    valid = (ck >= cs) & (ck < cs + KW)
    dc = np.clip(ck - cq, -(KW - 1), KW - 1) + KW - 1
    tz = jnp.where(jnp.asarray(valid)[None, None], rpb.astype(jnp.float32)[:, :, dc], NEG_BIG)
    dr = np.arange(KH)[:, None] + np.arange(KH)[None, :]
    tb = tz[:, dr]
    h = rpb.shape[0]
    return tb.transpose(0, 1, 3, 2, 4).reshape(h, KH, GRID_W, KH * GRID_W)


def _neighbourhood_attention(qkv, rpb, *, batch, seq, n_heads):
    t = qkv.shape[0]
    rows = seq // GRID_W
    bias = _attn_bias_table(rpb)
    blk = lambda off: pl.BlockSpec((seq, HEAD_DIM), lambda b, h: (b, off + h))
    return pl.pallas_call(
        functools.partial(_attn_body, rows=rows, scale=HEAD_DIM ** -0.5),
        grid=(batch, n_heads),
        in_specs=[blk(0), blk(n_heads), blk(2 * n_heads),
                  pl.BlockSpec((None, KH, GRID_W, KH * GRID_W), lambda b, h: (h, 0, 0, 0))],
        out_specs=pl.BlockSpec((seq, HEAD_DIM), lambda b, h: (b, h)),
        out_shape=jax.ShapeDtypeStruct((t, n_heads * HEAD_DIM), jnp.bfloat16),
        compiler_params=_params(2), name="natten")(qkv, qkv, qkv, bias)


def _top16(sc, ids):
    vals, idxs = [], []
    for _ in range(TOPK):
        m = jnp.max(sc, axis=0, keepdims=True)
        am = jnp.min(jnp.where(sc == m, ids, float(TOPK * TOPK)), axis=0, keepdims=True)
        vals.append(m)
        idxs.append(am)
        sc = jnp.where(ids == am, -jnp.inf, sc)
    return jnp.concatenate(vals, axis=0), jnp.concatenate(idxs, axis=0)


def _stair_rows():
    rows = [(0, j, True) for j in range(8)] + [(0, j, True) for j in range(8, 16)]
    rows += [(1, j, True) for j in range(8)]
    rows += [(i, 0, True) for i in range(8, 16)]
    rows += [(i, 1, i >= 2) for i in range(8)]
    rows += [(i, 0, i >= 2) for i in range(8)]
    rows += [(2, 2, True), (2, 3, True), (2, 4, True), (3, 2, True), (3, 3, True), (4, 2, True),
             (0, 0, False), (0, 1, False)]
    want = {(i, j) for i in range(TOPK) for j in range(TOPK) if (i + 1) * (j + 1) <= TOPK}
    got = [(i, j) for i, j, ok in rows if ok]
    assert len(got) == len(set(got)) and set(got) == want
    flat = np.array([i * TOPK + j for i, j, _ in rows], np.float32)
    mask = np.array([0.0 if ok else -np.inf for _, _, ok in rows], np.float32)
    return flat, mask


def _stair_candidates(v1, v2):
    tail = jnp.concatenate([v1[2:3] + v2[2:5], v1[3:4] + v2[2:4], v1[4:5] + v2[2:3], v1[0:1] + v2[0:2]], axis=0)
    return jnp.concatenate([v1[0:1] + v2[0:8], v1[0:1] + v2[8:16], v1[1:2] + v2[0:8], v1[8:16] + v2[0:1],
                            v1[0:8] + v2[1:2], v1[0:8] + v2[0:1], tail], axis=0)


def _pick_rows(table, sel):
    out = jnp.zeros_like(sel)
    for r in range(TOPK):
        out = jnp.where(sel == float(r), table[r:r + 1, :], out)
    return out


def _peer_topk_body(q_ref, sk_ref, flat_ref, mask_ref, i1_ref, i2_ref, g_ref, *, tt):
    nt = (((1,), (1,)), ((), ()))
    sc1 = lax.dot_general(sk_ref[0], q_ref[:, 0:D_HALF], nt, preferred_element_type=jnp.float32)
    sc2 = lax.dot_general(sk_ref[1], q_ref[:, D_HALF:2 * D_HALF], nt, preferred_element_type=jnp.float32)
    key_ids = lax.broadcasted_iota(jnp.int32, sc1.shape, 0).astype(jnp.float32)
    v1, x1 = _top16(sc1, key_ids)
    v2, x2 = _top16(sc2, key_ids)
    rep = tt // LANES
    flat = jnp.concatenate([flat_ref[...]] * rep, axis=1)
    cand = _stair_candidates(v1, v2) + jnp.concatenate([mask_ref[...]] * rep, axis=1)
    cv, ci = _top16(cand, flat)
    hi = jnp.floor(ci * (1.0 / TOPK))
    lo = ci - hi * TOPK
    i1_ref[...] = _pick_rows(x1, hi)
    i2_ref[...] = _pick_rows(x2, lo)
    e = jnp.exp(cv - cv[0:1, :])
    g_ref[...] = e / jnp.sum(e, axis=0, keepdims=True)


def _peer_topk(qp, subkeys, *, tt=512):
    t = qp.shape[0]
    flat, mask = _stair_rows()
    nrows = flat.shape[0]
    flat = jnp.asarray(np.tile(flat[:, None], (1, LANES)))
    mask = jnp.asarray(np.tile(mask[:, None], (1, LANES)))
    out = jax.ShapeDtypeStruct((PEER_HEADS * TOPK, t), jnp.float32)
    ospec = pl.BlockSpec((TOPK, tt), lambda i, h: (h, i))
    const = pl.BlockSpec((nrows, LANES), lambda i, h: (0, 0))
    return pl.pallas_call(
        functools.partial(_peer_topk_body, tt=tt),
        grid=(t // tt, PEER_HEADS),
        in_specs=[pl.BlockSpec((tt, 2 * D_HALF), lambda i, h: (i, h)),
                  pl.BlockSpec((None, 2, N_KEYS, D_HALF), lambda i, h: (h, 0, 0, 0)),
                  const, const],
        out_specs=[ospec, ospec, ospec], out_shape=[out, out, out],
        compiler_params=_params(2), name="peer_topk")(qp, subkeys, flat, mask)


G_PITCH = N_KEYS + SUBLANES


def _gate_matrix_body(i1_ref, i2_ref, g_ref, o_ref, i1s, i2s, gs, gpair, *, tt):
    i1s[...] = i1_ref[...].T
    i2s[...] = i2_ref[...].T
    gs[...] = g_ref[...].T
    ns = PEER_HEADS * TOPK
    key = lax.broadcasted_iota(jnp.int32, (N_KEYS, ns), 0).astype(jnp.float32)

    def token_gate(t):
        a = jnp.broadcast_to(i1s[pl.ds(t, 1), :], (N_KEYS, ns))
        b = jnp.broadcast_to(i2s[pl.ds(t, 1), :], (N_KEYS, ns))
        g = jnp.broadcast_to(gs[pl.ds(t, 1), :], (N_KEYS, ns))
        p = jnp.where(key == a, g, 0.0).astype(jnp.bfloat16)
        q = jnp.where(key == b, 1.0, 0.0).astype(jnp.bfloat16)
        return lax.dot_general(p, q, (((1,), (1,)), ((), ())), preferred_element_type=jnp.float32)

    def one_pair(pi, carry):
        lo = lax.bitcast_convert_type(token_gate(2 * pi).astype(jnp.bfloat16).astype(jnp.float32), jnp.uint32)
        hi = lax.bitcast_convert_type(token_gate(2 * pi + 1).astype(jnp.bfloat16).astype(jnp.float32), jnp.uint32)
        packed = (hi & jnp.uint32(0xFFFF0000)) | (lo >> 16)
        gpair[pl.ds(pl.multiple_of(pi * G_PITCH, SUBLANES), N_KEYS), :] = packed
        return carry

    lax.fori_loop(0, tt // 2, one_pair, 0, unroll=16)

    def one_key(a, carry):
        o_ref[a] = gpair[pl.ds(a, tt // 2, stride=G_PITCH), :]
        return carry

    lax.fori_loop(0, N_KEYS, one_key, 0, unroll=2)


def _gate_matrix(i1, i2, gate, *, tt=256):
    ns, t = i1.shape
    spec = pl.BlockSpec((ns, tt), lambda i: (0, i))
    return pl.pallas_call(
        functools.partial(_gate_matrix_body, tt=tt),
        grid=(t // tt,),
        in_specs=[spec, spec, spec],
        out_specs=pl.BlockSpec((N_KEYS, tt // 2, N_KEYS), lambda i: (0, i, 0)),
        out_shape=jax.ShapeDtypeStruct((N_KEYS, t // 2, N_KEYS), jnp.uint32),
        scratch_shapes=[pltpu.VMEM((tt, ns), jnp.float32)] * 3
        + [pltpu.VMEM((tt // 2 * G_PITCH, N_KEYS), jnp.uint32)],
        compiler_params=_params(1), name="peer_gate_matrix")(i1, i2, gate)


def _peer_dense_body(x_ref, u_ref, v_ref, g_ref, o_ref, *, na):
    @pl.when(pl.program_id(1) == 0)
    def _():
        o_ref[...] = jnp.zeros_like(o_ref)

    act = lax.dot_general(x_ref[...], u_ref[...], (((1,), (1,)), ((), ())), preferred_element_type=jnp.float32)
    g = jnp.concatenate([pltpu.bitcast(g_ref[k], jnp.bfloat16) for k in range(na)], axis=1)
    hval = (g.astype(jnp.float32) * _gelu(act)).astype(jnp.bfloat16)
    o_ref[...] += jnp.dot(hval, v_ref[...], preferred_element_type=jnp.float32)


def _peer_dense(x, u, v, g, *, tt=512, te=512):
    t, d = x.shape
    e = u.shape[0]
    na = te // N_KEYS
    return pl.pallas_call(
        functools.partial(_peer_dense_body, na=na),
        grid=(t // tt, e // te),
        in_specs=[pl.BlockSpec((tt, d), lambda i, j: (i, 0)),
                  pl.BlockSpec((te, d), lambda i, j: (j, 0)),
                  pl.BlockSpec((te, d), lambda i, j: (j, 0)),
                  pl.BlockSpec((na, tt // 2, N_KEYS), lambda i, j: (j, i, 0))],
        out_specs=pl.BlockSpec((tt, d), lambda i, j: (i, 0)),
        out_shape=jax.ShapeDtypeStruct((t, d), jnp.float32),
        compiler_params=_params(2), name="peer_dense")(x, u, v, g)


def kernel(x, ln0_g, ln0_b, w_in, conv_w, conv_b, lru_wa, lru_ba, lru_wx, lru_bx, lru_lambda, rpb, w_branch_rnn, w_branch_attn, w_out, ln1_g, ln1_b, peer_wq, peer_subkeys, peer_u, peer_v, ln2_g, ln2_b):
    bf16 = jnp.bfloat16
    batch, seq, d_model = x.shape
    t = batch * seq
    d_rnn = conv_w.shape[-1]
    d_attn = w_branch_attn.shape[1]
    n_heads = d_attn // HEAD_DIM
    s1, s2 = d_rnn, 2 * d_rnn
    s5 = s2 + 3 * d_attn

    h, h16 = _layer_norm(x.reshape(t, d_model), None, ln0_g, ln0_b)
    for l in range(DEPTH):
        w_in16 = w_in[l].astype(bf16)
        xr = _matmul(h16, w_in16, 0, s1, name="w_in_x")
        gg = _matmul(h16, w_in16, s1, s2 - s1, epilogue="gelu", out_dtype=bf16, name="w_in_gate")
        qkv = _matmul(h16, w_in16, s2, s5 - s2, out_dtype=bf16, name="w_in_qkv")
        mg = _matmul(h16, w_in16, s5, 2 * d_model, epilogue="sigmoid", out_dtype=bf16, name="w_in_merge")

        wg = jnp.concatenate([lru_wa[l, 0], lru_wx[l, 0], lru_wa[l, 1], lru_wx[l, 1]], axis=-1).astype(bf16)
        lh = d_rnn // LRU_BLOCK
        bg = jnp.concatenate([lru_ba[l, 0].reshape(lh, 1, LRU_BLOCK), lru_bx[l, 0].reshape(lh, 1, LRU_BLOCK),
                              lru_ba[l, 1].reshape(lh, 1, LRU_BLOCK), lru_bx[l, 1].reshape(lh, 1, LRU_BLOCK)], axis=-1)
        hr = _rg_lru(xr, gg, conv_w[l], conv_b[l].reshape(1, d_rnn), wg, bg, lru_lambda[l], batch=batch, seq=seq)
        at = _neighbourhood_attention(qkv, rpb[l], batch=batch, seq=seq, n_heads=n_heads)

        mixed = _branch_merge(hr, at, w_branch_rnn[l].astype(bf16), w_branch_attn[l].astype(bf16), mg)
        proj = _matmul(mixed, w_out[l].astype(bf16), 0, d_model, name="w_out")
        h, h16 = _layer_norm(h, proj, ln1_g[l], ln1_b[l], alpha=ALPHA)

        qp = _matmul(h16, peer_wq[l].astype(bf16), 0, peer_wq.shape[-1], out_dtype=bf16, name="peer_query")
        i1, i2, gate = _peer_topk(qp, peer_subkeys[l].astype(bf16))
        gmat = _gate_matrix(i1, i2, gate)
        y = _peer_dense(h16, peer_u[l].astype(bf16), peer_v[l].astype(bf16), gmat)
        (h,) = _layer_norm(h, y, ln2_g[l], ln2_b[l], alpha=ALPHA, want_bf16=False)
    return h.reshape(batch, seq, d_model)
```

```python
import functools
import math

import numpy as np
import jax
import jax.numpy as jnp
from jax import lax
from jax.experimental import pallas as pl
from jax.experimental.pallas import tpu as pltpu

LRU_BLOCK = 128
LRU_C = 8.0
CONV_W = 4
HEAD_DIM = 128
GRID_W = 64
KH = 8
KW = 16
ROW_GROUP = 16
PEER_HEADS = 8
N_KEYS = 128
TOPK = 16
D_HALF = 128
LN_EPS = 1e-5
DEPTH = 1
ALPHA = (2 * DEPTH) ** 0.25

V7X_VMEM_LIMIT_BYTES = 56 * 1024 * 1024
SUBLANES = 8
LANES = 128

NEG_BIG = -1e30
SQRT_HALF = 0.7071067811865476


def _params(n_axes, vmem=V7X_VMEM_LIMIT_BYTES):
    return pltpu.CompilerParams(dimension_semantics=("arbitrary",) * n_axes, vmem_limit_bytes=vmem)


def _gelu(x):
    return 0.5 * x * (1.0 + lax.erf(x * SQRT_HALF))


def _sigmoid(x):
    return 0.5 * jnp.tanh(0.5 * x) + 0.5


def _ln_body(*refs, alpha, has_res, has_bf16):
    if has_res:
        h_ref, r_ref, g_ref, b_ref = refs[:4]
        outs = refs[4:]
        x = alpha * h_ref[...] + r_ref[...]
    else:
        h_ref, g_ref, b_ref = refs[:3]
        outs = refs[3:]
        x = h_ref[...]
    mu = jnp.mean(x, axis=-1, keepdims=True)
    xc = x - mu
    var = jnp.mean(xc * xc, axis=-1, keepdims=True)
    y = xc * lax.rsqrt(var + LN_EPS) * g_ref[...] + b_ref[...]
    outs[0][...] = y
    if has_bf16:
        outs[1][...] = y.astype(jnp.bfloat16)


def _layer_norm(h, res, g, b, *, alpha=1.0, want_bf16=True, tm=256):
    t, d = h.shape
    row = pl.BlockSpec((tm, d), lambda i: (i, 0))
    vec = pl.BlockSpec((1, d), lambda i: (0, 0))
    has_res = res is not None
    ins = [h] + ([res] if has_res else []) + [g.reshape(1, d), b.reshape(1, d)]
    in_specs = [row] + ([row] if has_res else []) + [vec, vec]
    out_shape = [jax.ShapeDtypeStruct((t, d), jnp.float32)]
    out_specs = [row]
    if want_bf16:
        out_shape.append(jax.ShapeDtypeStruct((t, d), jnp.bfloat16))
        out_specs.append(row)
    outs = pl.pallas_call(
        functools.partial(_ln_body, alpha=alpha, has_res=has_res, has_bf16=want_bf16),
        grid=(t // tm,), in_specs=in_specs, out_specs=out_specs, out_shape=out_shape,
        compiler_params=_params(1), name="layer_norm")(*ins)
    return outs


def _mm_body(a_ref, b_ref, o_ref, *, epilogue):
    acc = jnp.dot(a_ref[...], b_ref[...], preferred_element_type=jnp.float32)
    if epilogue == "gelu":
        acc = _gelu(acc)
    elif epilogue == "sigmoid":
        acc = _sigmoid(acc)
    o_ref[...] = acc.astype(o_ref.dtype)


def _matmul(a, w, col0, ncols, *, epilogue=None, out_dtype=jnp.float32, tm=1024, tn=1024, name="matmul"):
    m, k = a.shape
    tm = min(tm, m)
    tn = math.gcd(tn, ncols, col0) if col0 else math.gcd(tn, ncols)
    cb0 = col0 // tn
    return pl.pallas_call(
        functools.partial(_mm_body, epilogue=epilogue),
        grid=(m // tm, ncols // tn),
        in_specs=[pl.BlockSpec((tm, k), lambda i, j: (i, 0)),
                  pl.BlockSpec((k, tn), lambda i, j: (0, cb0 + j))],
        out_specs=pl.BlockSpec((tm, tn), lambda i, j: (i, j)),
        out_shape=jax.ShapeDtypeStruct((m, ncols), out_dtype),
        compiler_params=_params(2), name=name)(a, w)


def _merge_body(hr_ref, at_ref, wr_ref, wa_ref, gr_ref, ga_ref, o_ref):
    yr = jnp.dot(hr_ref[...], wr_ref[...], preferred_element_type=jnp.float32)
    ya = jnp.dot(at_ref[...], wa_ref[...], preferred_element_type=jnp.float32)
    o_ref[...] = (gr_ref[...].astype(jnp.float32) * yr + ga_ref[...].astype(jnp.float32) * ya).astype(o_ref.dtype)


def _branch_merge(hr, at, w_rnn, w_attn, gates, *, tm=1024, tn=1024):
    m, kr = hr.shape
    ka = at.shape[1]
    n = w_rnn.shape[1]
    tm = min(tm, m)
    tn = min(tn, n)
    nb = n // tn
    return pl.pallas_call(
        _merge_body,
        grid=(m // tm, nb),
        in_specs=[pl.BlockSpec((tm, kr), lambda i, j: (i, 0)),
                  pl.BlockSpec((tm, ka), lambda i, j: (i, 0)),
                  pl.BlockSpec((kr, tn), lambda i, j: (0, j)),
                  pl.BlockSpec((ka, tn), lambda i, j: (0, j)),
                  pl.BlockSpec((tm, tn), lambda i, j: (i, j)),
                  pl.BlockSpec((tm, tn), lambda i, j: (i, nb + j))],
        out_specs=pl.BlockSpec((tm, tn), lambda i, j: (i, j)),
        out_shape=jax.ShapeDtypeStruct((m, n), jnp.bfloat16),
        compiler_params=_params(2), name="branch_merge")(hr, at, w_rnn, w_attn, gates, gates)


def _lru_body(x_ref, gg_ref, cw_ref, cb_ref, wg_ref, bg_ref, lam_ref, o_ref,
              xpad, af, uf, ab, ub, hs, *, seq, cblk, chunk):
    nh = cblk // LRU_BLOCK
    pad = SUBLANES
    xpad[0:pad, :] = jnp.zeros((pad, cblk), jnp.float32)
    xpad[pad + seq:2 * pad + seq, :] = jnp.zeros((pad, cblk), jnp.float32)
    xpad[pad:pad + seq, :] = x_ref[...]
    coef = -LRU_C * jax.nn.softplus(-lam_ref[...])

    def gate_chunk(c, carry):
        t0 = pl.multiple_of(c * chunk, chunk)
        win = xpad[pl.ds(t0, chunk + 2 * pad), :]
        xc = cb_ref[...]
        for kk in range(CONV_W):
            off = pad + kk - CONV_W // 2
            xc = xc + cw_ref[kk:kk + 1, :] * win[off:off + chunk, :]
        for hh in range(nh):
            lo, hi = hh * LRU_BLOCK, (hh + 1) * LRU_BLOCK
            xh = xc[:, lo:hi]
            g = jnp.dot(xh.astype(jnp.bfloat16), wg_ref[hh], preferred_element_type=jnp.float32) + bg_ref[hh]
            for d, (a_s, u_s) in enumerate(((af, uf), (ab, ub))):
                r = _sigmoid(g[:, (2 * d) * LRU_BLOCK:(2 * d + 1) * LRU_BLOCK])
                i = _sigmoid(g[:, (2 * d + 1) * LRU_BLOCK:(2 * d + 2) * LRU_BLOCK])
                a = jnp.exp(coef[d:d + 1, lo:hi] * r)
                u = jnp.sqrt(1.0 - a * a) * (i * xh)
                a_s[pl.ds(t0, chunk), lo:hi] = a
                u_s[pl.ds(t0, chunk), lo:hi] = u
        return carry

    lax.fori_loop(0, seq // chunk, gate_chunk, 0)

    row = lax.broadcasted_iota(jnp.int32, (SUBLANES, cblk), 0)
    ngroups = seq // SUBLANES

    def fwd_group(gi, carry):
        r0 = pl.multiple_of(gi * SUBLANES, SUBLANES)
        a = af[pl.ds(r0, SUBLANES), :]
        h = uf[pl.ds(r0, SUBLANES), :]
        for d in (1, 2, 4):
            a_sh = jnp.where(row >= d, pltpu.roll(a, d, 0), 1.0)
            h_sh = jnp.where(row >= d, pltpu.roll(h, d, 0), 0.0)
            h = a * h_sh + h
            a = a * a_sh
        h = h + a * carry
        hs[pl.ds(r0, SUBLANES), :] = h
        return h[SUBLANES - 1:SUBLANES, :]

    lax.fori_loop(0, ngroups, fwd_group, jnp.zeros((1, cblk), jnp.float32), unroll=4)

    def bwd_group(gi, carry):
        r0 = pl.multiple_of((ngroups - 1 - gi) * SUBLANES, SUBLANES)
        a = ab[pl.ds(r0, SUBLANES), :]
        h = ub[pl.ds(r0, SUBLANES), :]
        for d in (1, 2, 4):
            a_sh = jnp.where(row < SUBLANES - d, pltpu.roll(a, SUBLANES - d, 0), 1.0)
            h_sh = jnp.where(row < SUBLANES - d, pltpu.roll(h, SUBLANES - d, 0), 0.0)
            h = a * h_sh + h
            a = a * a_sh
        h = h + a * carry
        hs[pl.ds(r0, SUBLANES), :] = hs[pl.ds(r0, SUBLANES), :] + h
        return h[0:1, :]

    lax.fori_loop(0, ngroups, bwd_group, jnp.zeros((1, cblk), jnp.float32), unroll=4)

    def out_chunk(c, carry):
        t0 = pl.multiple_of(c * chunk, chunk)
        o_ref[pl.ds(t0, chunk), :] = (gg_ref[pl.ds(t0, chunk), :].astype(jnp.float32)
                                      * hs[pl.ds(t0, chunk), :]).astype(o_ref.dtype)
        return carry

    lax.fori_loop(0, seq // chunk, out_chunk, 0)


def _rg_lru(xr, gg, conv_w, conv_b, wg, bg, lam, *, batch, seq, cblk=256, chunk=256):
    t, d = xr.shape
    nh = cblk // LRU_BLOCK
    ncb = d // cblk
    tok = pl.BlockSpec((seq, cblk), lambda b, c: (b, c))
    f32 = jnp.float32
    return pl.pallas_call(
        functools.partial(_lru_body, seq=seq, cblk=cblk, chunk=chunk),
        grid=(batch, ncb),
        in_specs=[tok, tok,
                  pl.BlockSpec((CONV_W, cblk), lambda b, c: (0, c)),
                  pl.BlockSpec((1, cblk), lambda b, c: (0, c)),
                  pl.BlockSpec((nh, LRU_BLOCK, 4 * LRU_BLOCK), lambda b, c: (c, 0, 0)),
                  pl.BlockSpec((nh, 1, 4 * LRU_BLOCK), lambda b, c: (c, 0, 0)),
                  pl.BlockSpec((2, cblk), lambda b, c: (0, c))],
        out_specs=tok,
        out_shape=jax.ShapeDtypeStruct((t, d), jnp.bfloat16),
        scratch_shapes=[pltpu.VMEM((seq + 2 * SUBLANES, cblk), f32)] + [pltpu.VMEM((seq, cblk), f32)] * 5,
        compiler_params=_params(2), name="rg_lru")(xr, gg, conv_w, conv_b, wg, bg, lam)


def _attn_body(q_ref, k_ref, v_ref, bias_ref, o_ref, *, rows, scale):
    band = KH * GRID_W

    def row_group(rg, carry):
        pos, scores, probs = [], [], []
        for u in range(ROW_GROUP):
            r = rg * ROW_GROUP + u
            rs = jnp.clip(r - KH // 2, 0, rows - KH)
            q0 = pl.multiple_of(r * GRID_W, GRID_W)
            k0 = pl.multiple_of(rs * GRID_W, GRID_W)
            pos.append((q0, k0))
            s = lax.dot_general(q_ref[pl.ds(q0, GRID_W), :], k_ref[pl.ds(k0, band), :],
                                (((1,), (1,)), ((), ())), preferred_element_type=jnp.float32)
            scores.append(s * scale + bias_ref[rs - r + KH - 1])
        for s in scores:
            p = jnp.exp(s - jnp.max(s, axis=-1, keepdims=True))
            probs.append((p.astype(jnp.bfloat16), jnp.sum(p, axis=-1, keepdims=True)))
        for (q0, k0), (p, l) in zip(pos, probs):
            o = jnp.dot(p, v_ref[pl.ds(k0, band), :], preferred_element_type=jnp.float32) / l
            o_ref[pl.ds(q0, GRID_W), :] = o.astype(o_ref.dtype)
        return carry

    lax.fori_loop(0, rows // ROW_GROUP, row_group, 0)


def _attn_bias_table(rpb):
    cq = np.arange(GRID_W)[:, None]
    ck = np.arange(GRID_W)[None, :]
    cs = np.clip(cq - KW // 2, 0, GRID_W - KW)
    valid = (ck >= cs) & (ck < cs + KW)
    dc = np.clip(ck - cq, -(KW - 1), KW - 1) + KW - 1
    tz = jnp.where(jnp.asarray(valid)[None, None], rpb.astype(jnp.float32)[:, :, dc], NEG_BIG)
    dr = np.arange(KH)[:, None] + np.arange(KH)[None, :]
    tb = tz[:, dr]
    h = rpb.shape[0]
    return tb.transpose(0, 1, 3, 2, 4).reshape(h, KH, GRID_W, KH * GRID_W)


def _neighbourhood_attention(qkv, rpb, *, batch, seq, n_heads):
    t = qkv.shape[0]
    rows = seq // GRID_W
    bias = _attn_bias_table(rpb)
    blk = lambda off: pl.BlockSpec((seq, HEAD_DIM), lambda b, h: (b, off + h))
    return pl.pallas_call(
        functools.partial(_attn_body, rows=rows, scale=HEAD_DIM ** -0.5),
        grid=(batch, n_heads),
        in_specs=[blk(0), blk(n_heads), blk(2 * n_heads),
                  pl.BlockSpec((None, KH, GRID_W, KH * GRID_W), lambda b, h: (h, 0, 0, 0))],
        out_specs=pl.BlockSpec((seq, HEAD_DIM), lambda b, h: (b, h)),
        out_shape=jax.ShapeDtypeStruct((t, n_heads * HEAD_DIM), jnp.bfloat16),
        compiler_params=_params(2), name="natten")(qkv, qkv, qkv, bias)


def _top16(sc, ids):
    vals, idxs = [], []
    for _ in range(TOPK):
        m = jnp.max(sc, axis=0, keepdims=True)
        am = jnp.min(jnp.where(sc == m, ids, float(TOPK * TOPK)), axis=0, keepdims=True)
        vals.append(m)
        idxs.append(am)
        sc = jnp.where(ids == am, -jnp.inf, sc)
    return jnp.concatenate(vals, axis=0), jnp.concatenate(idxs, axis=0)


def _stair_rows():
    rows = [(0, j, True) for j in range(8)] + [(0, j, True) for j in range(8, 16)]
    rows += [(1, j, True) for j in range(8)]
    rows += [(i, 0, True) for i in range(8, 16)]
    rows += [(i, 1, i >= 2) for i in range(8)]
    rows += [(i, 0, i >= 2) for i in range(8)]
    rows += [(2, 2, True), (2, 3, True), (2, 4, True), (3, 2, True), (3, 3, True), (4, 2, True),
             (0, 0, False), (0, 1, False)]
    want = {(i, j) for i in range(TOPK) for j in range(TOPK) if (i + 1) * (j + 1) <= TOPK}
    got = [(i, j) for i, j, ok in rows if ok]
    assert len(got) == len(set(got)) and set(got) == want
    flat = np.array([i * TOPK + j for i, j, _ in rows], np.float32)
    mask = np.array([0.0 if ok else -np.inf for _, _, ok in rows], np.float32)
    return flat, mask


def _stair_candidates(v1, v2):
    tail = jnp.concatenate([v1[2:3] + v2[2:5], v1[3:4] + v2[2:4], v1[4:5] + v2[2:3], v1[0:1] + v2[0:2]], axis=0)
    return jnp.concatenate([v1[0:1] + v2[0:8], v1[0:1] + v2[8:16], v1[1:2] + v2[0:8], v1[8:16] + v2[0:1],
                            v1[0:8] + v2[1:2], v1[0:8] + v2[0:1], tail], axis=0)


def _pick_rows(table, sel):
    out = jnp.zeros_like(sel)
    for r in range(TOPK):
        out = jnp.where(sel == float(r), table[r:r + 1, :], out)
    return out


def _peer_topk_body(q_ref, sk_ref, flat_ref, mask_ref, i1_ref, i2_ref, g_ref, *, tt):
    nt = (((1,), (1,)), ((), ()))
    sc1 = lax.dot_general(sk_ref[0], q_ref[:, 0:D_HALF], nt, preferred_element_type=jnp.float32)
    sc2 = lax.dot_general(sk_ref[1], q_ref[:, D_HALF:2 * D_HALF], nt, preferred_element_type=jnp.float32)
    key_ids = lax.broadcasted_iota(jnp.int32, sc1.shape, 0).astype(jnp.float32)
    v1, x1 = _top16(sc1, key_ids)
    v2, x2 = _top16(sc2, key_ids)
    rep = tt // LANES
    flat = jnp.concatenate([flat_ref[...]] * rep, axis=1)
    cand = _stair_candidates(v1, v2) + jnp.concatenate([mask_ref[...]] * rep, axis=1)
    cv, ci = _top16(cand, flat)
    hi = jnp.floor(ci * (1.0 / TOPK))
    lo = ci - hi * TOPK
    i1_ref[...] = _pick_rows(x1, hi)
    i2_ref[...] = _pick_rows(x2, lo)
    e = jnp.exp(cv - cv[0:1, :])
    g_ref[...] = e / jnp.sum(e, axis=0, keepdims=True)


def _peer_topk(qp, subkeys, *, tt=512):
    t = qp.shape[0]
    flat, mask = _stair_rows()
    nrows = flat.shape[0]
    flat = jnp.asarray(np.tile(flat[:, None], (1, LANES)))
    mask = jnp.asarray(np.tile(mask[:, None], (1, LANES)))
    out = jax.ShapeDtypeStruct((PEER_HEADS * TOPK, t), jnp.float32)
    ospec = pl.BlockSpec((TOPK, tt), lambda i, h: (h, i))
    const = pl.BlockSpec((nrows, LANES), lambda i, h: (0, 0))
    return pl.pallas_call(
        functools.partial(_peer_topk_body, tt=tt),
        grid=(t // tt, PEER_HEADS),
        in_specs=[pl.BlockSpec((tt, 2 * D_HALF), lambda i, h: (i, h)),
                  pl.BlockSpec((None, 2, N_KEYS, D_HALF), lambda i, h: (h, 0, 0, 0)),
                  const, const],
        out_specs=[ospec, ospec, ospec], out_shape=[out, out, out],
        compiler_params=_params(2), name="peer_topk")(qp, subkeys, flat, mask)


G_PITCH = N_KEYS + SUBLANES


def _gate_matrix_body(i1_ref, i2_ref, g_ref, o_ref, i1s, i2s, gs, gpair, *, tt):
    i1s[...] = i1_ref[...].T
    i2s[...] = i2_ref[...].T
    gs[...] = g_ref[...].T
    ns = PEER_HEADS * TOPK
    key = lax.broadcasted_iota(jnp.int32, (N_KEYS, ns), 0).astype(jnp.float32)

    def token_gate(t):
        a = jnp.broadcast_to(i1s[pl.ds(t, 1), :], (N_KEYS, ns))
        b = jnp.broadcast_to(i2s[pl.ds(t, 1), :], (N_KEYS, ns))
        g = jnp.broadcast_to(gs[pl.ds(t, 1), :], (N_KEYS, ns))
        p = jnp.where(key == a, g, 0.0).astype(jnp.bfloat16)
        q = jnp.where(key == b, 1.0, 0.0).astype(jnp.bfloat16)
        return lax.dot_general(p, q, (((1,), (1,)), ((), ())), preferred_element_type=jnp.float32)

    def one_pair(pi, carry):
        lo = lax.bitcast_convert_type(token_gate(2 * pi).astype(jnp.bfloat16).astype(jnp.float32), jnp.uint32)
        hi = lax.bitcast_convert_type(token_gate(2 * pi + 1).astype(jnp.bfloat16).astype(jnp.float32), jnp.uint32)
        packed = (hi & jnp.uint32(0xFFFF0000)) | (lo >> 16)
        gpair[pl.ds(pl.multiple_of(pi * G_PITCH, SUBLANES), N_KEYS), :] = packed
        return carry

    lax.fori_loop(0, tt // 2, one_pair, 0, unroll=16)

    def one_key(a, carry):
        o_ref[a] = gpair[pl.ds(a, tt // 2, stride=G_PITCH), :]
        return carry

    lax.fori_loop(0, N_KEYS, one_key, 0, unroll=2)


def _gate_matrix(i1, i2, gate, *, tt=256):
    ns, t = i1.shape
    spec = pl.BlockSpec((ns, tt), lambda i: (0, i))
    return pl.pallas_call(
        functools.partial(_gate_matrix_body, tt=tt),
        grid=(t // tt,),
        in_specs=[spec, spec, spec],
        out_specs=pl.BlockSpec((N_KEYS, tt // 2, N_KEYS), lambda i: (0, i, 0)),
        out_shape=jax.ShapeDtypeStruct((N_KEYS, t // 2, N_KEYS), jnp.uint32),
        scratch_shapes=[pltpu.VMEM((tt, ns), jnp.float32)] * 3
        + [pltpu.VMEM((tt // 2 * G_PITCH, N_KEYS), jnp.uint32)],
        compiler_params=_params(1), name="peer_gate_matrix")(i1, i2, gate)


def _peer_dense_body(x_ref, u_ref, v_ref, g_ref, o_ref, *, na):
    @pl.when(pl.program_id(1) == 0)
    def _():
        o_ref[...] = jnp.zeros_like(o_ref)

    act = lax.dot_general(x_ref[...], u_ref[...], (((1,), (1,)), ((), ())), preferred_element_type=jnp.float32)
    g = jnp.concatenate([pltpu.bitcast(g_ref[k], jnp.bfloat16) for k in range(na)], axis=1)
    hval = (g.astype(jnp.float32) * _gelu(act)).astype(jnp.bfloat16)
    o_ref[...] += jnp.dot(hval, v_ref[...], preferred_element_type=jnp.float32)


def _peer_dense(x, u, v, g, *, tt=512, te=512):
    t, d = x.shape
    e = u.shape[0]
    na = te // N_KEYS
    return pl.pallas_call(
        functools.partial(_peer_dense_body, na=na),
        grid=(t // tt, e // te),
        in_specs=[pl.BlockSpec((tt, d), lambda i, j: (i, 0)),
                  pl.BlockSpec((te, d), lambda i, j: (j, 0)),
                  pl.BlockSpec((te, d), lambda i, j: (j, 0)),
                  pl.BlockSpec((na, tt // 2, N_KEYS), lambda i, j: (j, i, 0))],
        out_specs=pl.BlockSpec((tt, d), lambda i, j: (i, 0)),
        out_shape=jax.ShapeDtypeStruct((t, d), jnp.float32),
        compiler_params=_params(2), name="peer_dense")(x, u, v, g)


def kernel(x, ln0_g, ln0_b, w_in, conv_w, conv_b, lru_wa, lru_ba, lru_wx, lru_bx, lru_lambda, rpb, w_branch_rnn, w_branch_attn, w_out, ln1_g, ln1_b, peer_wq, peer_subkeys, peer_u, peer_v, ln2_g, ln2_b):
    bf16 = jnp.bfloat16
    batch, seq, d_model = x.shape
    t = batch * seq
    d_rnn = conv_w.shape[-1]
    d_attn = w_branch_attn.shape[1]
    n_heads = d_attn // HEAD_DIM
    s1, s2 = d_rnn, 2 * d_rnn
    s5 = s2 + 3 * d_attn

    h, h16 = _layer_norm(x.reshape(t, d_model), None, ln0_g, ln0_b)
    for l in range(DEPTH):
        w_in16 = w_in[l].astype(bf16)
        xr = _matmul(h16, w_in16, 0, s1, name="w_in_x")
        gg = _matmul(h16, w_in16, s1, s2 - s1, epilogue="gelu", out_dtype=bf16, name="w_in_gate")
        qkv = _matmul(h16, w_in16, s2, s5 - s2, out_dtype=bf16, name="w_in_qkv")
        mg = _matmul(h16, w_in16, s5, 2 * d_model, epilogue="sigmoid", out_dtype=bf16, name="w_in_merge")

        wg = jnp.concatenate([lru_wa[l, 0], lru_wx[l, 0], lru_wa[l, 1], lru_wx[l, 1]], axis=-1).astype(bf16)
        lh = d_rnn // LRU_BLOCK
        bg = jnp.concatenate([lru_ba[l, 0].reshape(lh, 1, LRU_BLOCK), lru_bx[l, 0].reshape(lh, 1, LRU_BLOCK),
                              lru_ba[l, 1].reshape(lh, 1, LRU_BLOCK), lru_bx[l, 1].reshape(lh, 1, LRU_BLOCK)], axis=-1)
        hr = _rg_lru(xr, gg, conv_w[l], conv_b[l].reshape(1, d_rnn), wg, bg, lru_lambda[l], batch=batch, seq=seq)
        at = _neighbourhood_attention(qkv, rpb[l], batch=batch, seq=seq, n_heads=n_heads)

        mixed = _branch_merge(hr, at, w_branch_rnn[l].astype(bf16), w_branch_attn[l].astype(bf16), mg)
        proj = _matmul(mixed, w_out[l].astype(bf16), 0, d_model, name="w_out")
        h, h16 = _layer_norm(h, proj, ln1_g[l], ln1_b[l], alpha=ALPHA)

        qp = _matmul(h16, peer_wq[l].astype(bf16), 0, peer_wq.shape[-1], out_dtype=bf16, name="peer_query")
        i1, i2, gate = _peer_topk(qp, peer_subkeys[l].astype(bf16))
        gmat = _gate_matrix(i1, i2, gate)
        y = _peer_dense(h16, peer_u[l].astype(bf16), peer_v[l].astype(bf16), gmat)
        (h,) = _layer_norm(h, y, ln2_g[l], ln2_b[l], alpha=ALPHA, want_bf16=False)
    return h.reshape(batch, seq, d_model)
```
